```python
import jax, jax.numpy as jnp
from jax import lax
import numpy as np

D_MODEL = 1024
BATCH = 2
SEQ = 16384
DEPTH = 2
DEC_BATCH = 32
DEC_SEQ = 2048
PAST_LEN = 128

N_EVEN = (DEPTH + 1) // 2
N_ODD = DEPTH // 2

CONV_CH = D_MODEL // 2
CONV_WIDTH = 31
ATT_HEADS = 8
ATT_HEAD_DIM = 64
ATT_WIDTH = ATT_HEADS * ATT_HEAD_DIM
DILATED_PATTERNS = ((128, 1), (512, 4), (2048, 16))
N_PATTERNS = len(DILATED_PATTERNS)
ROPE_THETA = 500000.0
ROPE_DIM = ATT_HEAD_DIM // 4
AB_IN = 2 * CONV_CH + N_PATTERNS * 3 * ATT_WIDTH
AB_OUT = CONV_CH + ATT_WIDTH
MLSTM_HEADS = 8
MLSTM_WIDTH = D_MODEL
MLSTM_HEAD_DIM = MLSTM_WIDTH // MLSTM_HEADS
MLSTM_CHUNK = 128
C_IN = 4 * MLSTM_WIDTH + 4 * MLSTM_HEADS
D_FF = 4 * D_MODEL

EPS = 1e-6
NEG_INF = -1e30

kernel_name = "hybrid_conv_dilated_mlstm_encoder"


def rms_norm(x, g):
    xf = x.astype(jnp.float32)
    y = xf * lax.rsqrt(jnp.mean(xf * xf, axis=-1, keepdims=True) + EPS)
    return (y * g.astype(jnp.float32)).astype(x.dtype)


def layer_norm(x, g, b):
    xf = x.astype(jnp.float32)
    mu = jnp.mean(xf, axis=-1, keepdims=True)
    xc = xf - mu
    y = xc * lax.rsqrt(jnp.mean(xc * xc, axis=-1, keepdims=True) + EPS)
    return (y * g.astype(jnp.float32) + b.astype(jnp.float32)).astype(x.dtype)


def partial_rope(x, positions):
    half = ROPE_DIM // 2
    inv_freq = jnp.power(ROPE_THETA, -jnp.arange(half, dtype=jnp.float32) / half)
    ang = positions.astype(jnp.float32)[:, None] * inv_freq[None, :]
    cos = jnp.cos(ang)[None, :, None, :]
    sin = jnp.sin(ang)[None, :, None, :]
    x1 = x[..., :half]
    x2 = x[..., half:ROPE_DIM]
    rot = jnp.concatenate([x1 * cos - x2 * sin, x2 * cos + x1 * sin], axis=-1)
    return jnp.concatenate([rot, x[..., ROPE_DIM:]], axis=-1)


def dilated_window_attention(q, k, v, window, dilation):
    N, S, H, hd = q.shape
    radius = window // (2 * dilation)
    L = S // dilation
    blk = radius
    nb = -(-L // blk)
    Lp = nb * blk
    Nd = N * dilation

    def to_sub(t):
        return t.reshape(N, L, dilation, H, hd).transpose(0, 2, 1, 3, 4).reshape(Nd, L, H, hd)

    qs, ks, vs = to_sub(q), to_sub(k), to_sub(v)
    qb = jnp.pad(qs, ((0, 0), (0, Lp - L), (0, 0), (0, 0))).reshape(Nd, nb, blk, H, hd)

    def neighbours(t):
        tp = jnp.pad(t, ((0, 0), (blk, Lp - L + blk), (0, 0), (0, 0))).reshape(Nd, nb + 2, blk, H, hd)
        return jnp.concatenate([tp[:, :-2], tp[:, 1:-1], tp[:, 2:]], axis=2)

    kw, vw = neighbours(ks), neighbours(vs)
    s = jnp.einsum('nbqhd,nbkhd->nbhqk', qb, kw) * (hd ** -0.5)
    qpos = jnp.arange(nb)[:, None] * blk + jnp.arange(blk)[None, :]
    kpos = (jnp.arange(nb)[:, None] - 1) * blk + jnp.arange(3 * blk)[None, :]
    kp = kpos[:, None, :]
    mask = (jnp.abs(qpos[:, :, None] - kp) <= radius) & (kp >= 0) & (kp < L)
    s = jnp.where(mask[None, :, None], s, NEG_INF)
    m = jnp.max(s, axis=-1, keepdims=True)
    p = jnp.exp(s - m)
    l = jnp.sum(p, axis=-1, keepdims=True)
    o = jnp.einsum('nbhqk,nbkhd->nbqhd', p, vw) / l.transpose(0, 1, 3, 2, 4)
    lse = (m + jnp.log(l))[..., 0].transpose(0, 1, 3, 2)
    o = o.reshape(Nd, Lp, H, hd)[:, :L]
    lse = lse.reshape(Nd, Lp, H)[:, :L]
    o = o.reshape(N, dilation, L, H, hd).transpose(0, 2, 1, 3, 4).reshape(N, S, H, hd)
    lse = lse.reshape(N, dilation, L, H).transpose(0, 2, 1, 3).reshape(N, S, H)
    return o, lse


def conv_attn_mixer(h, w_in, conv_w, conv_b, ln_g, ln_b, w_out):
    N, S, _ = h.shape
    proj = h @ w_in
    a_lin = proj[..., :CONV_CH]
    a_gate = proj[..., CONV_CH:2 * CONV_CH]
    qkv = proj[..., 2 * CONV_CH:]
    a = a_lin * jax.nn.sigmoid(a_gate)
    a = lax.conv_general_dilated(
        a, conv_w[:, None, :].astype(a.dtype), window_strides=(1,),
        padding=[(CONV_WIDTH // 2, CONV_WIDTH // 2)],
        dimension_numbers=('NWC', 'WIO', 'NWC'), feature_group_count=CONV_CH) + conv_b
    a = jax.nn.silu(layer_norm(a, ln_g, ln_b))
    qkv = qkv.reshape(N, S, N_PATTERNS, 3, ATT_HEADS, ATT_HEAD_DIM).astype(jnp.float32)
    pos = jnp.arange(S)
    outs, lses = [], []
    for g, (window, dilation) in enumerate(DILATED_PATTERNS):
        q = partial_rope(qkv[:, :, g, 0], pos)
        k = partial_rope(qkv[:, :, g, 1], pos)
        o, lse = dilated_window_attention(q, k, qkv[:, :, g, 2], window, dilation)
        outs.append(o)
        lses.append(lse)
    wts = jax.nn.softmax(jnp.stack(lses, axis=0), axis=0)
    att = jnp.einsum('pnsh,pnshd->nshd', wts, jnp.stack(outs, axis=0))
    att = att.reshape(N, S, ATT_WIDTH).astype(h.dtype)
    return jnp.concatenate([a.astype(h.dtype), att], axis=-1) @ w_out


def mlstm_chunk_scan(q, k, v, ig, lf):
    N, H, S, d = q.shape
    L = MLSTM_CHUNK
    nc = S // L

    def chunks(t):
        return jnp.moveaxis(t.reshape(N, H, nc, L, *t.shape[3:]), 2, 0)

    causal = jnp.tril(jnp.ones((L, L), dtype=bool))

    def step(carry, xs):
        C, n, m = carry
        qc, kc, vc, igc, lfc = xs
        b = jnp.cumsum(lfc, axis=-1)
        logD = jnp.where(causal, b[..., :, None] - b[..., None, :] + igc[..., None, :], NEG_INF)
        inter = b + m[..., None]
        mt = jnp.maximum(inter, jnp.max(logD, axis=-1))
        sc = jnp.einsum('nhtd,nhsd->nhts', qc, kc) * jnp.exp(logD - mt[..., None])
        ei = jnp.exp(inter - mt)
        num = jnp.einsum('nhts,nhse->nhte', sc, vc) + ei[..., None] * jnp.einsum('nhtd,nhde->nhte', qc, C)
        den = jnp.sum(sc, axis=-1) + ei * jnp.einsum('nhtd,nhd->nht', qc, n)
        hc = num / jnp.maximum(jnp.abs(den), jnp.exp(-mt))[..., None]
        bL = b[..., -1]
        wlog = bL[..., None] - b + igc
        m_new = jnp.maximum(bL + m, jnp.max(wlog, axis=-1))
        decay = jnp.exp(bL + m - m_new)
        w = jnp.exp(wlog - m_new[..., None])
        kw = kc * w[..., None]
        C_new = decay[..., None, None] * C + jnp.einsum('nhsd,nhse->nhde', kw, vc)
        n_new = decay[..., None] * n + jnp.sum(kw, axis=2)
        return (C_new, n_new, m_new), hc

    init = (jnp.zeros((N, H, d, d), jnp.float32), jnp.zeros((N, H, d), jnp.float32),
            jnp.zeros((N, H), jnp.float32))
    _, hs = lax.scan(step, init, (chunks(q), chunks(k), chunks(v), chunks(ig), chunks(lf)))
    return jnp.moveaxis(hs, 0, 2).reshape(N, H, S, d)


def mlstm_mixer(h, w_in, gate_b, head_norm, w_out):
    N, S, _ = h.shape
    W, H, dh = MLSTM_WIDTH, MLSTM_HEADS, MLSTM_HEAD_DIM
    proj = (h @ w_in).astype(jnp.float32)

    def heads(t):
        return t.reshape(N, S, H, dh).transpose(0, 2, 1, 3)

    q = heads(proj[..., :W])
    k = heads(proj[..., W:2 * W]) * (dh ** -0.5)
    v = heads(proj[..., 2 * W:3 * W])
    o = proj[..., 3 * W:4 * W]
    gates = (proj[..., 4 * W:] + gate_b.astype(jnp.float32)).reshape(N, S, 4, H).transpose(2, 0, 3, 1)
    ig_f, f_f, ig_b, f_b = gates[0], gates[1], gates[2], gates[3]

    def flip(t):
        return jnp.flip(t, axis=2)

    qq = jnp.concatenate([q, flip(q)], axis=0)
    kk = jnp.concatenate([k, flip(k)], axis=0)
    vv = jnp.concatenate([v, flip(v)], axis=0)
    ig = jnp.concatenate([ig_f, flip(ig_b)], axis=0)
    lf = jnp.concatenate([jax.nn.log_sigmoid(f_f), flip(jax.nn.log_sigmoid(f_b))], axis=0)
    hh = mlstm_chunk_scan(qq, kk, vv, ig, lf)
    ht = hh[:N] + flip(hh[N:])
    ht = ht * lax.rsqrt(jnp.mean(ht * ht, axis=-1, keepdims=True) + EPS)
    ht = ht.transpose(0, 2, 1, 3).reshape(N, S, W) * head_norm.astype(jnp.float32)
    return (jax.nn.sigmoid(o) * ht).astype(h.dtype) @ w_out


def squared_relu_mlp(h, w1, w2):
    return jnp.square(jax.nn.relu(h @ w1)) @ w2


def trunk(x, c, ada_w, ada_b, norm_mix, norm_mlp, ab_w_in, conv_w, conv_b, conv_ln_g,
          conv_ln_b, ab_w_out, c_w_in, c_gate_b, c_head_norm, c_w_out, mlp_w1, mlp_w2,
          norm_final):
    for layer in range(DEPTH):
        mod = jax.nn.silu(c.astype(jnp.float32)) @ ada_w[layer].astype(jnp.float32) + ada_b[layer].astype(jnp.float32)
        mod = mod.astype(x.dtype)[:, None, :]
        sh1, sc1, g1, sh2, sc2, g2 = jnp.split(mod, 6, axis=-1)
        h = rms_norm(x, norm_mix[layer]) * (1 + sc1) + sh1
        i = layer // 2
        if layer % 2 == 0:
            mix = conv_attn_mixer(h, ab_w_in[i], conv_w[i], conv_b[i], conv_ln_g[i],
                                  conv_ln_b[i], ab_w_out[i])
        else:
            mix = mlstm_mixer(h, c_w_in[i], c_gate_b[i], c_head_norm[i], c_w_out[i])
        x = x + g1 * mix
        h = rms_norm(x, norm_mlp[layer]) * (1 + sc2) + sh2
        x = x + g2 * squared_relu_mlp(h, mlp_w1[layer], mlp_w2[layer])
    return rms_norm(x, norm_final)


def setup_inputs(seed: int = 0) -> dict:
    key = jax.random.key(seed)
    ks = jax.random.split(key, 26)

    def nrm(k, shape, scale):
        return jax.random.normal(k, shape, jnp.float32) * scale

    D = D_MODEL
    c_main = nrm(ks[14], (N_ODD, D, 4 * MLSTM_WIDTH), D ** -0.5)
    c_gates = nrm(ks[15], (N_ODD, D, 4 * MLSTM_HEADS), 0.1 * D ** -0.5)
    ig_bias = nrm(ks[16], (N_ODD, 2, MLSTM_HEADS), 0.1)
    f_bias = 3.0 + 3.0 * jax.random.uniform(ks[17], (N_ODD, 2, MLSTM_HEADS), jnp.float32)
    gate_b = jnp.stack([ig_bias[:, 0], f_bias[:, 0], ig_bias[:, 1], f_bias[:, 1]], axis=1)
    return {
        "x_prompt": nrm(ks[0], (BATCH, SEQ, D), 1.0),
        "x_sample": nrm(ks[1], (DEC_BATCH, DEC_SEQ, D), 1.0),
        "c_prompt": nrm(ks[2], (BATCH, D), 1.0),
        "c_sample": nrm(ks[3], (DEC_BATCH, D), 1.0),
        "ada_w": nrm(ks[4], (DEPTH, D, 6 * D), D ** -0.5),
        "ada_b": nrm(ks[5], (DEPTH, 6 * D), 0.02),
        "norm_mix": 1.0 + nrm(ks[6], (DEPTH, D), 0.05),
        "norm_mlp": 1.0 + nrm(ks[7], (DEPTH, D), 0.05),
        "ab_w_in": nrm(ks[8], (N_EVEN, D, AB_IN), D ** -0.5),
        "conv_w": nrm(ks[9], (N_EVEN, CONV_WIDTH, CONV_CH), CONV_WIDTH ** -0.5),
        "conv_b": nrm(ks[10], (N_EVEN, CONV_CH), 0.02),
        "conv_ln_g": 1.0 + nrm(ks[11], (N_EVEN, CONV_CH), 0.05),
        "conv_ln_b": nrm(ks[12], (N_EVEN, CONV_CH), 0.02),
        "ab_w_out": nrm(ks[13], (N_EVEN, AB_OUT, D), AB_OUT ** -0.5),
        "c_w_in": jnp.concatenate([c_main, c_gates], axis=-1),
        "c_gate_b": gate_b.reshape(N_ODD, 4 * MLSTM_HEADS),
        "c_head_norm": 1.0 + nrm(ks[18], (N_ODD, MLSTM_WIDTH), 0.05),
        "c_w_out": nrm(ks[19], (N_ODD, MLSTM_WIDTH, D), MLSTM_WIDTH ** -0.5),
        "mlp_w1": nrm(ks[20], (DEPTH, D, D_FF), D ** -0.5),
        "mlp_w2": nrm(ks[21], (DEPTH, D_FF, D), D_FF ** -0.5),
        "norm_final": 1.0 + nrm(ks[22], (D,), 0.05),
    }


def reference(x_prompt, x_sample, c_prompt, c_sample, ada_w, ada_b, norm_mix, norm_mlp,
              ab_w_in, conv_w, conv_b, conv_ln_g, conv_ln_b, ab_w_out, c_w_in, c_gate_b,
              c_head_norm, c_w_out, mlp_w1, mlp_w2, norm_final):
    y_prompt = trunk(x_prompt, c_prompt, ada_w, ada_b, norm_mix, norm_mlp, ab_w_in, conv_w,
                     conv_b, conv_ln_g, conv_ln_b, ab_w_out, c_w_in, c_gate_b, c_head_norm,
                     c_w_out, mlp_w1, mlp_w2, norm_final)
    y_sample = trunk(x_sample, c_sample, ada_w, ada_b, norm_mix, norm_mlp, ab_w_in, conv_w,
                     conv_b, conv_ln_g, conv_ln_b, ab_w_out, c_w_in, c_gate_b, c_head_norm,
                     c_w_out, mlp_w1, mlp_w2, norm_final)
    return (y_prompt, y_sample)
```

```python
import functools

import jax
import jax.numpy as jnp
from jax import lax
from jax.experimental import pallas as pl
from jax.experimental.pallas import tpu as pltpu

D_MODEL = 1024
DEPTH = 2
CONV_CH = 512
CONV_WIDTH = 31
CONV_HALO = 16
ATT_HEADS = 8
ATT_HEAD_DIM = 64
ATT_WIDTH = ATT_HEADS * ATT_HEAD_DIM
DILATED_PATTERNS = ((128, 1), (512, 4), (2048, 16))
N_PATTERNS = len(DILATED_PATTERNS)
ATT_RADIUS = 64
ROPE_THETA = 500000.0
ROPE_DIM = ATT_HEAD_DIM // 4
ROPE_HALF = ROPE_DIM // 2
AB_IN = 2 * CONV_CH + N_PATTERNS * 3 * ATT_WIDTH
QKV_W = N_PATTERNS * 3 * ATT_WIDTH
MLSTM_HEADS = 8
MLSTM_WIDTH = D_MODEL
MLSTM_HEAD_DIM = MLSTM_WIDTH // MLSTM_HEADS
MLSTM_CHUNK = 128
N_GATES = 4 * MLSTM_HEADS
GATE_PAD = 128
D_FF = 4 * D_MODEL
EPS = 1e-6
NEG_INF = -1e30

LANES = 128
VMEM_LIMIT = 56 * 1024 * 1024
ROW_TILE = 512
ATT_TQ = 128
CONV_ROWS = 32
MLP_FF_TILE = 1024

F32 = jnp.float32
BF16 = jnp.bfloat16


def _params(*sem):
    return pltpu.CompilerParams(dimension_semantics=sem, vmem_limit_bytes=VMEM_LIMIT)


def _norm_mod(x, gain, scale, shift):
    y = x * lax.rsqrt(jnp.mean(x * x, axis=-1, keepdims=True) + EPS)
    return (y * gain) * (1.0 + scale) + shift


def _mod_kernel(c_ref, w_ref, b_ref, o_ref):
    c = c_ref[...]
    s = c * jax.nn.sigmoid(c)
    o_ref[...] = jnp.dot(s, w_ref[...], preferred_element_type=F32,
                         precision=lax.Precision.HIGHEST) + b_ref[...]


def _modulation(c_all, ada_w, ada_b):
    R = c_all.shape[0]
    D = D_MODEL
    out = pl.pallas_call(
        _mod_kernel,
        grid=(DEPTH, 6),
        in_specs=[
            pl.BlockSpec((R, D), lambda l, j: (0, 0)),
            pl.BlockSpec((None, D, D), lambda l, j: (l, 0, j)),
            pl.BlockSpec((None, 1, D), lambda l, j: (l, 0, j)),
        ],
        out_specs=pl.BlockSpec((None, R, D), lambda l, j: (l, 0, j)),
        out_shape=jax.ShapeDtypeStruct((DEPTH, R, 6 * D), F32),
        compiler_params=_params("arbitrary", "arbitrary"),
        name="adaln_mod",
    )(c_all, ada_w, ada_b.reshape(DEPTH, 1, 6 * D))
    return out.reshape(DEPTH, R, 6, D)


def _rope_tables(S):
    inv_freq = jnp.power(ROPE_THETA, -jnp.arange(ROPE_HALF, dtype=F32) / ROPE_HALF)
    ang = jnp.arange(S).astype(F32)[:, None] * inv_freq[None, :]
    cos, sin = jnp.cos(ang), jnp.sin(ang)
    ones = jnp.ones((S, ATT_HEAD_DIM - ROPE_DIM), F32)
    zeros = jnp.zeros((S, ATT_HEAD_DIM - ROPE_DIM), F32)
    zh = jnp.zeros((S, ROPE_HALF), F32)
    cos_t = jnp.concatenate([cos, cos, ones], axis=1)
    sin_up = jnp.concatenate([-sin, zh, zeros], axis=1)
    sin_dn = jnp.concatenate([zh, sin, zeros], axis=1)
    rep = LANES // ATT_HEAD_DIM
    return tuple(jnp.tile(t, (1, rep)) for t in (cos_t, sin_up, sin_dn))


def _ab_in_kernel(x_ref, mod_ref, gain_ref, w_ref, cos_ref, sup_ref, sdn_ref, a_ref, qkv_ref):
    x = x_ref[...]
    h = _norm_mod(x, gain_ref[...], mod_ref[1:2, :], mod_ref[0:1, :]).astype(BF16)

    def seg(j):
        return jnp.dot(h, w_ref[:, j * ATT_WIDTH:(j + 1) * ATT_WIDTH], preferred_element_type=F32)

    a_ref[...] = (seg(0) * jax.nn.sigmoid(seg(1))).astype(a_ref.dtype)

    cos_t, sin_up, sin_dn = cos_ref[...], sup_ref[...], sdn_ref[...]

    def rope(p, scale):
        parts = []
        for j in range(ATT_WIDTH // LANES):
            t = p[:, j * LANES:(j + 1) * LANES]
            up = pltpu.roll(t, LANES - ROPE_HALF, 1)
            dn = pltpu.roll(t, ROPE_HALF, 1)
            r = t * cos_t + up * sin_up + dn * sin_dn
            if scale != 1.0:
                r = r * scale
            parts.append(r.astype(qkv_ref.dtype))
        return jnp.concatenate(parts, axis=1)

    for g in range(N_PATTERNS):
        base = 3 * g * ATT_WIDTH
        qkv_ref[:, base:base + ATT_WIDTH] = rope(seg(2 + 3 * g), ATT_HEAD_DIM ** -0.5)
        qkv_ref[:, base + ATT_WIDTH:base + 2 * ATT_WIDTH] = rope(seg(3 + 3 * g), 1.0)
        qkv_ref[:, base + 2 * ATT_WIDTH:base + 3 * ATT_WIDTH] = seg(4 + 3 * g).astype(qkv_ref.dtype)


def _ab_in_proj(x, mod, n_off, gain, w_in, tables):
    N, S, D = x.shape
    tm = min(ROW_TILE, S)
    return pl.pallas_call(
        _ab_in_kernel,
        grid=(N, S // tm),
        in_specs=[
            pl.BlockSpec((None, tm, D), lambda n, i: (n, i, 0)),
            pl.BlockSpec((None, None, 6, D), lambda n, i: (0, n + n_off, 0, 0)),
            pl.BlockSpec((1, D), lambda n, i: (0, 0)),
            pl.BlockSpec((D, AB_IN), lambda n, i: (0, 0)),
            pl.BlockSpec((tm, LANES), lambda n, i: (i, 0)),
            pl.BlockSpec((tm, LANES), lambda n, i: (i, 0)),
            pl.BlockSpec((tm, LANES), lambda n, i: (i, 0)),
        ],
        out_specs=[
            pl.BlockSpec((None, tm, CONV_CH), lambda n, i: (n, i, 0)),
            pl.BlockSpec((None, tm, QKV_W), lambda n, i: (n, i, 0)),
        ],
        out_shape=[
            jax.ShapeDtypeStruct((N, S, CONV_CH), BF16),
            jax.ShapeDtypeStruct((N, S, QKV_W), BF16),
        ],
        compiler_params=_params("parallel", "parallel"),
        name="ab_in_proj",
    )(x, mod, gain, w_in, *tables)


def _attn_kernel(*refs, tq, n_blocks, has_prev, is_last):
    q_ref, kp_ref, kc_ref, kn_ref, vp_ref, vc_ref, vn_ref = refs[:7]
    refs = refs[7:]
    if has_prev:
        po_ref, pl_ref = refs[:2]
        refs = refs[2:]
    if is_last:
        (o_ref,) = refs
    else:
        o_ref, l_ref = refs

    i = pl.program_id(2)
    tk = tq + 2 * ATT_RADIUS
    row = lax.broadcasted_iota(jnp.int32, (tq, tk), 0)
    col = lax.broadcasted_iota(jnp.int32, (tq, tk), 1)
    rel = col - row
    lo = jnp.where(i == 0, ATT_RADIUS, 0)
    hi = jnp.where(i == n_blocks - 1, tq + ATT_RADIUS, tk)
    mask = (rel >= 0) & (rel <= 2 * ATT_RADIUS) & (col >= lo) & (col < hi)
    lane = lax.broadcasted_iota(jnp.int32, (1, LANES), 1)
    first_head = lane < ATT_HEAD_DIM

    for j in range(ATT_WIDTH // LANES):
        sl = slice(j * LANES, (j + 1) * LANES)
        q = q_ref[:, sl]
        kw = jnp.concatenate([kp_ref[:, sl], kc_ref[:, sl], kn_ref[:, sl]], axis=0)
        vw = jnp.concatenate([vp_ref[:, sl], vc_ref[:, sl], vn_ref[:, sl]], axis=0)
        o_pair = None
        lse_pair = None
        for hh in range(2):
            sel = first_head if hh == 0 else jnp.logical_not(first_head)
            qm = jnp.where(sel, q, jnp.zeros_like(q))
            vm = jnp.where(sel, vw, jnp.zeros_like(vw))
            s = lax.dot_general(qm, kw, (((1,), (1,)), ((), ())), preferred_element_type=F32)
            s = jnp.where(mask, s, NEG_INF)
            m = jnp.max(s, axis=-1, keepdims=True)
            p = jnp.exp(s - m)
            l = jnp.sum(p, axis=-1, keepdims=True)
            acc = jnp.dot(p.astype(vm.dtype), vm, preferred_element_type=F32)
            o_h = acc / l
            lse_h = m + jnp.log(l)
            if hh == 0:
                o_pair, lse_pair = o_h, lse_h
            else:
                o_pair = o_pair + o_h
                lse_pair = jnp.where(first_head, lse_pair, lse_h)
        if has_prev:
            lse_prev = pl_ref[:, sl]
            mx = jnp.maximum(lse_prev, lse_pair)
            e_prev = jnp.exp(lse_prev - mx)
            e_cur = jnp.exp(lse_pair - mx)
            tot = e_prev + e_cur
            o_pair = (po_ref[:, sl] * e_prev + o_pair * e_cur) / tot
            lse_pair = mx + jnp.log(tot)
        o_ref[:, sl] = o_pair.astype(o_ref.dtype)
        if not is_last:
            l_ref[:, sl] = lse_pair


def _dilated_attention(qkv, g, prev):
    N, S, _ = qkv.shape
    dil = DILATED_PATTERNS[g][1]
    L = S // dil
    tq = min(ATT_TQ, L)
    nb = L // tq
    hb = tq // ATT_RADIUS
    n_halo = L // ATT_RADIUS
    seg = QKV_W // ATT_WIDTH
    has_prev = prev is not None
    is_last = g == N_PATTERNS - 1

    qkv_v = qkv.reshape(N, L, dil * QKV_W)

    def cur(c):
        return pl.BlockSpec((None, tq, ATT_WIDTH), lambda n, r, i: (n, i, r * seg + 3 * g + c))

    def before(c):
        return pl.BlockSpec((None, ATT_RADIUS, ATT_WIDTH),
                            lambda n, r, i: (n, jnp.maximum(i * hb - 1, 0), r * seg + 3 * g + c))

    def after(c):
        return pl.BlockSpec((None, ATT_RADIUS, ATT_WIDTH),
                            lambda n, r, i: (n, jnp.minimum((i + 1) * hb, n_halo - 1), r * seg + 3 * g + c))

    state_spec = pl.BlockSpec((None, tq, ATT_WIDTH), lambda n, r, i: (n, i, r))
    in_specs = [cur(0), before(1), cur(1), after(1), before(2), cur(2), after(2)]
    args = [qkv_v] * 7
    if has_prev:
        in_specs += [state_spec, state_spec]
        args += [prev[0].reshape(N, L, dil * ATT_WIDTH), prev[1].reshape(N, L, dil * ATT_WIDTH)]
    if is_last:
        out_specs = [state_spec]
        out_shape = [jax.ShapeDtypeStruct((N, L, dil * ATT_WIDTH), BF16)]
    else:
        out_specs = [state_spec, state_spec]
        out_shape = [jax.ShapeDtypeStruct((N, L, dil * ATT_WIDTH), F32)] * 2
    outs = pl.pallas_call(
        functools.partial(_attn_kernel, tq=tq, n_blocks=nb, has_prev=has_prev, is_last=is_last),
        grid=(N, dil, nb),
        in_specs=in_specs,
        out_specs=out_specs,
        out_shape=out_shape,
        compiler_params=_params("parallel", "parallel", "parallel"),
        name=f"dilated_attn_{g}",
    )(*args)
    return tuple(o.reshape(N, S, ATT_WIDTH) for o in outs)


def _ab_out_kernel(ap_ref, ac_ref, an_ref, att_ref, x_ref, mod_ref, cw_ref, cb_ref, lg_ref, lb_ref,
                   w_ref, o_ref, ext_ref, act_ref, *, tm, n_tiles):
    i = pl.program_id(1)
    H = CONV_HALO
    ext_ref[0:H, :] = jnp.where(i > 0, ap_ref[...].astype(F32), 0.0)
    ext_ref[H:H + tm, :] = ac_ref[...].astype(F32)
    ext_ref[H + tm:H + tm + H, :] = jnp.where(i < n_tiles - 1, an_ref[...].astype(F32), 0.0)

    off = H - CONV_WIDTH // 2

    for c in range(tm // CONV_ROWS):
        r0 = c * CONV_ROWS
        acc = jnp.broadcast_to(cb_ref[...], (CONV_ROWS, CONV_CH))
        for j in range(CONV_WIDTH):
            acc = acc + cw_ref[j:j + 1, :] * ext_ref[r0 + off + j:r0 + off + j + CONV_ROWS, :]
        mu = jnp.mean(acc, axis=-1, keepdims=True)
        xc = acc - mu
        y = xc * lax.rsqrt(jnp.mean(xc * xc, axis=-1, keepdims=True) + EPS)
        y = y * lg_ref[...] + lb_ref[...]
        act_ref[r0:r0 + CONV_ROWS, :] = (y * jax.nn.sigmoid(y)).astype(act_ref.dtype)

    mix = jnp.dot(act_ref[...], w_ref[0:CONV_CH, :], preferred_element_type=F32)
    mix = mix + jnp.dot(att_ref[...], w_ref[CONV_CH:, :], preferred_element_type=F32)
    o_ref[...] = x_ref[...] + mod_ref[2:3, :] * mix


def _ab_out_proj(a, att, x, mod, n_off, conv_w, conv_b, ln_g, ln_b, w_out):
    N, S, D = x.shape
    tm = min(ROW_TILE, S)
    nt = S // tm
    hb = tm // CONV_HALO
    n_halo = S // CONV_HALO
    row = lambda n, i: (n, i, 0)
    const = lambda n, i: (0, 0)
    return pl.pallas_call(
        functools.partial(_ab_out_kernel, tm=tm, n_tiles=nt),
        grid=(N, nt),
        in_specs=[
            pl.BlockSpec((None, CONV_HALO, CONV_CH), lambda n, i: (n, jnp.maximum(i * hb - 1, 0), 0)),
            pl.BlockSpec((None, tm, CONV_CH), row),
            pl.BlockSpec((None, CONV_HALO, CONV_CH), lambda n, i: (n, jnp.minimum((i + 1) * hb, n_halo - 1), 0)),
            pl.BlockSpec((None, tm, ATT_WIDTH), row),
            pl.BlockSpec((None, tm, D), row),
            pl.BlockSpec((None, None, 6, D), lambda n, i: (0, n + n_off, 0, 0)),
            pl.BlockSpec((CONV_WIDTH, CONV_CH), const),
            pl.BlockSpec((1, CONV_CH), const),
            pl.BlockSpec((1, CONV_CH), const),
            pl.BlockSpec((1, CONV_CH), const),
            pl.BlockSpec((D, D), const),
        ],
        out_specs=pl.BlockSpec((None, tm, D), row),
        out_shape=jax.ShapeDtypeStruct((N, S, D), F32),
        scratch_shapes=[
            pltpu.VMEM((tm + 2 * CONV_HALO, CONV_CH), F32),
            pltpu.VMEM((tm, CONV_CH), BF16),
        ],
        compiler_params=_params("parallel", "parallel"),
        name="ab_out_proj",
    )(a, a, a, att, x, mod, conv_w, conv_b, ln_g, ln_b, w_out)


def _mlp_kernel(x_ref, mod_ref, gain_ref, w1_ref, w2_ref, *rest, final):
    if final:
        fg_ref, o_ref = rest
    else:
        (o_ref,) = rest
    x = x_ref[...]
    h = _norm_mod(x, gain_ref[...], mod_ref[4:5, :], mod_ref[3:4, :]).astype(BF16)
    acc = jnp.zeros(x.shape, F32)
    for c in range(D_FF // MLP_FF_TILE):
        sl = slice(c * MLP_FF_TILE, (c + 1) * MLP_FF_TILE)
        u = jnp.maximum(jnp.dot(h, w1_ref[:, sl], preferred_element_type=F32), 0.0)
        acc = acc + jnp.dot((u * u).astype(BF16), w2_ref[sl, :], preferred_element_type=F32)
    y = x + mod_ref[5:6, :] * acc
    if final:
        y = y * lax.rsqrt(jnp.mean(y * y, axis=-1, keepdims=True) + EPS) * fg_ref[...]
    o_ref[...] = y


def _mlp(x, mod, n_off, layer, gain, w1, w2, final_gain=None):
    N, S, D = x.shape
    tm = min(ROW_TILE, S)
    row = lambda n, i: (n, i, 0)
    const = lambda n, i: (0, 0)
    final = final_gain is not None
    in_specs = [
        pl.BlockSpec((None, tm, D), row),
        pl.BlockSpec((None, None, 6, D), lambda n, i: (layer, n + n_off, 0, 0)),
        pl.BlockSpec((1, D), const),
        pl.BlockSpec((D, D_FF), const),
        pl.BlockSpec((D_FF, D), const),
    ]
    args = [x, mod, gain, w1, w2]
    if final:
        in_specs.append(pl.BlockSpec((1, D), const))
        args.append(final_gain)
    return pl.pallas_call(
        functools.partial(_mlp_kernel, final=final),
        grid=(N, S // tm),
        in_specs=in_specs,
        out_specs=pl.BlockSpec((None, tm, D), row),
        out_shape=jax.ShapeDtypeStruct((N, S, D), F32),
        compiler_params=_params("parallel", "parallel"),
        name=f"mlp_{layer}",
    )(*args)


def _c_in_kernel(x_ref, mod_ref, gain_ref, w_ref, gb_ref, qkv_ref, og_ref, gate_ref):
    W = MLSTM_WIDTH
    x = x_ref[...]
    h = _norm_mod(x, gain_ref[...], mod_ref[1:2, :], mod_ref[0:1, :]).astype(BF16)

    def seg(lo, width):
        return jnp.dot(h, w_ref[:, lo:lo + width], preferred_element_type=F32)

    qkv_ref[:, 0:W] = seg(0, W).astype(qkv_ref.dtype)
    qkv_ref[:, W:2 * W] = (seg(W, W) * (MLSTM_HEAD_DIM ** -0.5)).astype(qkv_ref.dtype)
    qkv_ref[:, 2 * W:3 * W] = seg(2 * W, W).astype(qkv_ref.dtype)
    og_ref[...] = jax.nn.sigmoid(seg(3 * W, W)).astype(og_ref.dtype)
    z = seg(4 * W, GATE_PAD) + gb_ref[...]
    lane = lax.broadcasted_iota(jnp.int32, (1, GATE_PAD), 1)
    is_forget = (lane % (2 * MLSTM_HEADS)) >= MLSTM_HEADS
    log_sig = jnp.minimum(z, 0.0) - jnp.log1p(jnp.exp(-jnp.abs(z)))
    gate_ref[...] = jnp.where(is_forget, log_sig, z)


def _c_in_proj(x, mod, n_off, gain, w_in, gate_b):
    N, S, D = x.shape
    W = MLSTM_WIDTH
    tm = min(ROW_TILE, S)
    row = lambda n, i: (n, i, 0)
    const = lambda n, i: (0, 0)
    return pl.pallas_call(
        _c_in_kernel,
        grid=(N, S // tm),
        in_specs=[
            pl.BlockSpec((None, tm, D), row),
            pl.BlockSpec((None, None, 6, D), lambda n, i: (1, n + n_off, 0, 0)),
            pl.BlockSpec((1, D), const),
            pl.BlockSpec((D, 4 * W + GATE_PAD), const),
            pl.BlockSpec((1, GATE_PAD), const),
        ],
        out_specs=[
            pl.BlockSpec((None, tm, 3 * W), row),
            pl.BlockSpec((None, tm, W), row),
            pl.BlockSpec((None, tm, GATE_PAD), row),
        ],
        out_shape=[
            jax.ShapeDtypeStruct((N, S, 3 * W), BF16),
            jax.ShapeDtypeStruct((N, S, W), BF16),
            jax.ShapeDtypeStruct((N, S, GATE_PAD), F32),
        ],
        compiler_params=_params("parallel", "parallel"),
        name="c_in_proj",
    )(x, mod, gain, w_in, gate_b)


def _split3(x):
    hi = x.astype(BF16)
    r1 = x - hi.astype(F32)
    mid = r1.astype(BF16)
    lo = (r1 - mid.astype(F32)).astype(BF16)
    return hi, mid, lo


def _mlstm_kernel(qf_ref, kf_ref, vf_ref, gf_ref, qb_ref, kb_ref, vb_ref, gb_ref,
                  hf_ref, hb_ref, c_ref, n_ref, m_ref):
    L = MLSTM_CHUNK
    dh = MLSTM_HEAD_DIM

    @pl.when(pl.program_id(1) == 0)
    def _():
        c_ref[...] = jnp.zeros_like(c_ref)
        n_ref[...] = jnp.zeros_like(n_ref)
        m_ref[...] = jnp.zeros_like(m_ref)

    row = lax.broadcasted_iota(jnp.int32, (L, L), 0)
    col = lax.broadcasted_iota(jnp.int32, (L, L), 1)

    dirs = (
        (qf_ref, kf_ref, vf_ref, gf_ref, hf_ref, row >= col, L - 1),
        (qb_ref, kb_ref, vb_ref, gb_ref, hb_ref, row <= col, 0),
    )
    for d, (q_ref, k_ref, v_ref, g_ref, h_ref, allowed, last) in enumerate(dirs):
        gates = g_ref[...]
        tri = allowed.astype(BF16)
        csum = sum(jnp.dot(tri, piece, preferred_element_type=F32) for piece in _split3(gates))
        gates_t = gates.T
        csum_t = csum.T
        for hd in range(MLSTM_HEADS):
            ci = 2 * MLSTM_HEADS * d + hd
            cf = ci + MLSTM_HEADS
            sl = slice(hd * dh, (hd + 1) * dh)
            q = q_ref[:, sl]
            k = k_ref[:, sl]
            v = v_ref[:, sl]
            C = c_ref[d, hd]
            n = n_ref[d, hd]
            m = m_ref[d, hd][:, 0:1]
            b_col = csum[:, cf:cf + 1]
            b_row = csum_t[cf:cf + 1, :]
            ig_col = gates[:, ci:ci + 1]
            ig_row = gates_t[ci:ci + 1, :]
            log_d = jnp.where(allowed, b_col - b_row + ig_row, NEG_INF)
            inter = b_col + m
            mt = jnp.maximum(inter, jnp.max(log_d, axis=-1, keepdims=True))
            s = lax.dot_general(q, k, (((1,), (1,)), ((), ())), preferred_element_type=F32)
            sc = s * jnp.exp(log_d - mt)
            ei = jnp.exp(inter - mt)
            num = jnp.dot(sc.astype(BF16), v, preferred_element_type=F32)
            num = num + ei * jnp.dot(q, C.astype(BF16), preferred_element_type=F32)
            qn = jnp.sum(q.astype(F32) * n, axis=-1, keepdims=True)
            den = jnp.sum(sc, axis=-1, keepdims=True) + ei * qn
            h_ref[:, sl] = (num / jnp.maximum(jnp.abs(den), jnp.exp(-mt))).astype(h_ref.dtype)

            b_last = csum[last:last + 1, cf:cf + 1]
            m_new = jnp.maximum(b_last + m, jnp.max(b_last - b_row + ig_row, axis=-1, keepdims=True))
            decay = jnp.exp(b_last + m - m_new)
            w_col = jnp.exp(b_last - b_col + ig_col - m_new)
            kw = k.astype(F32) * w_col
            upd = lax.dot_general(kw.astype(BF16), v, (((0,), (0,)), ((), ())), preferred_element_type=F32)
            c_ref[d, hd] = decay * C + upd
            n_ref[d, hd] = decay * n + jnp.sum(kw, axis=0, keepdims=True)
            m_ref[d, hd] = jnp.broadcast_to(m_new, (1, LANES))


def _mlstm_scan(qkv, gates):
    N, S, _ = qkv.shape
    W = MLSTM_WIDTH
    L = MLSTM_CHUNK
    nc = S // L
    fwd = lambda c: (lambda n, i: (n, i, c))
    bwd = lambda c: (lambda n, i: (n, nc - 1 - i, c))
    blk = lambda im: pl.BlockSpec((None, L, W), im)
    gblk = lambda im: pl.BlockSpec((None, L, GATE_PAD), im)
    return pl.pallas_call(
        _mlstm_kernel,
        grid=(N, nc),
        in_specs=[blk(fwd(0)), blk(fwd(1)), blk(fwd(2)), gblk(fwd(0)),
                  blk(bwd(0)), blk(bwd(1)), blk(bwd(2)), gblk(bwd(0))],
        out_specs=[blk(fwd(0)), blk(bwd(0))],
        out_shape=[jax.ShapeDtypeStruct((N, S, W), F32)] * 2,
        scratch_shapes=[
            pltpu.VMEM((2, MLSTM_HEADS, MLSTM_HEAD_DIM, MLSTM_HEAD_DIM), F32),
            pltpu.VMEM((2, MLSTM_HEADS, 1, MLSTM_HEAD_DIM), F32),
            pltpu.VMEM((2, MLSTM_HEADS, 1, LANES), F32),
        ],
        compiler_params=_params("parallel", "arbitrary"),
        name="mlstm_scan",
    )(qkv, qkv, qkv, gates, qkv, qkv, qkv, gates)


def _c_out_kernel(hf_ref, hb_ref, og_ref, x_ref, mod_ref, hn_ref, w_ref, o_ref):
    dh = MLSTM_HEAD_DIM
    parts = []
    for hd in range(MLSTM_HEADS):
        sl = slice(hd * dh, (hd + 1) * dh)
        ht = hf_ref[:, sl].astype(F32) + hb_ref[:, sl].astype(F32)
        ht = ht * lax.rsqrt(jnp.mean(ht * ht, axis=-1, keepdims=True) + EPS)
        parts.append((og_ref[:, sl].astype(F32) * (ht * hn_ref[:, sl])).astype(BF16))
    z = jnp.concatenate(parts, axis=1)
    mix = jnp.dot(z, w_ref[...], preferred_element_type=F32)
    o_ref[...] = x_ref[...] + mod_ref[2:3, :] * mix


def _c_out_proj(hf, hb, og, x, mod, n_off, head_norm, w_out):
    N, S, D = x.shape
    W = MLSTM_WIDTH
    tm = min(ROW_TILE, S)
    row = lambda n, i: (n, i, 0)
    const = lambda n, i: (0, 0)
    return pl.pallas_call(
        _c_out_kernel,
        grid=(N, S // tm),
        in_specs=[
            pl.BlockSpec((None, tm, W), row),
            pl.BlockSpec((None, tm, W), row),
            pl.BlockSpec((None, tm, W), row),
            pl.BlockSpec((None, tm, D), row),
            pl.BlockSpec((None, None, 6, D), lambda n, i: (1, n + n_off, 0, 0)),
            pl.BlockSpec((1, W), const),
            pl.BlockSpec((W, D), const),
        ],
        out_specs=pl.BlockSpec((None, tm, D), row),
        out_shape=jax.ShapeDtypeStruct((N, S, D), F32),
        compiler_params=_params("parallel", "parallel"),
        name="c_out_proj",
    )(hf, hb, og, x, mod, head_norm, w_out)


def _trunk(x, mod, n_off, p):
    N, S, D = x.shape
    tables = _rope_tables(S)
    a, qkv = _ab_in_proj(x, mod, n_off, p["norm_mix0"], p["ab_w_in"], tables)
    state = None
    for g in range(N_PATTERNS):
        state = _dilated_attention(qkv, g, state)
    x = _ab_out_proj(a, state[0], x, mod, n_off, p["conv_w"], p["conv_b"], p["conv_ln_g"],
                     p["conv_ln_b"], p["ab_w_out"])
    x = _mlp(x, mod, n_off, 0, p["norm_mlp0"], p["mlp_w1_0"], p["mlp_w2_0"])
    qkv1, og, gates = _c_in_proj(x, mod, n_off, p["norm_mix1"], p["c_w_in"], p["c_gate_b"])
    hf, hb = _mlstm_scan(qkv1, gates)
    x = _c_out_proj(hf, hb, og, x, mod, n_off, p["c_head_norm"], p["c_w_out"])
    return _mlp(x, mod, n_off, 1, p["norm_mlp1"], p["mlp_w1_1"], p["mlp_w2_1"], p["norm_final"])


def kernel(x_prompt, x_sample, c_prompt, c_sample, ada_w, ada_b, norm_mix, norm_mlp, ab_w_in, conv_w, conv_b, conv_ln_g, conv_ln_b, ab_w_out, c_w_in, c_gate_b, c_head_norm, c_w_out, mlp_w1, mlp_w2, norm_final):
    D = D_MODEL
    n_p, n_s = c_prompt.shape[0], c_sample.shape[0]
    rows = n_p + n_s
    rows_pad = -(-rows // 8) * 8
    c_all = jnp.concatenate([c_prompt, c_sample, jnp.zeros((rows_pad - rows, D), F32)], axis=0)
    mod = _modulation(c_all, ada_w, ada_b)

    W = MLSTM_WIDTH
    pad = GATE_PAD - N_GATES
    p = {
        "norm_mix0": norm_mix[0].reshape(1, D),
        "norm_mix1": norm_mix[1].reshape(1, D),
        "norm_mlp0": norm_mlp[0].reshape(1, D),
        "norm_mlp1": norm_mlp[1].reshape(1, D),
        "ab_w_in": ab_w_in[0].astype(BF16),
        "conv_w": conv_w[0],
        "conv_b": conv_b[0].reshape(1, CONV_CH),
        "conv_ln_g": conv_ln_g[0].reshape(1, CONV_CH),
        "conv_ln_b": conv_ln_b[0].reshape(1, CONV_CH),
        "ab_w_out": ab_w_out[0].astype(BF16),
        "c_w_in": jnp.pad(c_w_in[0], ((0, 0), (0, pad))).astype(BF16),
        "c_gate_b": jnp.pad(c_gate_b[0], (0, pad)).reshape(1, GATE_PAD),
        "c_head_norm": c_head_norm[0].reshape(1, W),
        "c_w_out": c_w_out[0].astype(BF16),
        "mlp_w1_0": mlp_w1[0].astype(BF16),
        "mlp_w2_0": mlp_w2[0].astype(BF16),
        "mlp_w1_1": mlp_w1[1].astype(BF16),
        "mlp_w2_1": mlp_w2[1].astype(BF16),
        "norm_final": norm_final.reshape(1, D),
    }
    y_prompt = _trunk(x_prompt, mod, 0, p)
    y_sample = _trunk(x_sample, mod, n_p, p)
    return (y_prompt, y_sample)
```

```python
import functools

import jax
import jax.numpy as jnp
from jax import lax
from jax.experimental import pallas as pl
from jax.experimental.pallas import tpu as pltpu

D_MODEL = 1024
DEPTH = 2
CONV_CH = 512
CONV_WIDTH = 31
CONV_HALO = 16
ATT_HEADS = 8
ATT_HEAD_DIM = 64
ATT_WIDTH = ATT_HEADS * ATT_HEAD_DIM
DILATED_PATTERNS = ((128, 1), (512, 4), (2048, 16))
N_PATTERNS = len(DILATED_PATTERNS)
ATT_RADIUS = 64
ROPE_THETA = 500000.0
ROPE_DIM = ATT_HEAD_DIM // 4
ROPE_HALF = ROPE_DIM // 2
AB_IN = 2 * CONV_CH + N_PATTERNS * 3 * ATT_WIDTH
QKV_W = N_PATTERNS * 3 * ATT_WIDTH
MLSTM_HEADS = 8
MLSTM_WIDTH = D_MODEL
MLSTM_HEAD_DIM = MLSTM_WIDTH // MLSTM_HEADS
MLSTM_CHUNK = 128
N_GATES = 4 * MLSTM_HEADS
GATE_PAD = 128
D_FF = 4 * D_MODEL
EPS = 1e-6
NEG_INF = -1e30

LANES = 128
VMEM_LIMIT = 56 * 1024 * 1024
ROW_TILE = 512
ATT_TQ = 128
CONV_ROWS = 32
MLP_FF_TILE = 1024

F32 = jnp.float32
BF16 = jnp.bfloat16


def _params(*sem):
    return pltpu.CompilerParams(dimension_semantics=sem, vmem_limit_bytes=VMEM_LIMIT)


def _norm_mod(x, gain, scale, shift):
    y = x * lax.rsqrt(jnp.mean(x * x, axis=-1, keepdims=True) + EPS)
    return (y * gain) * (1.0 + scale) + shift


def _mod_kernel(c_ref, w_ref, b_ref, o_ref):
    c = c_ref[...]
    s = c * jax.nn.sigmoid(c)
    o_ref[...] = jnp.dot(s, w_ref[...], preferred_element_type=F32,
                         precision=lax.Precision.HIGHEST) + b_ref[...]


def _modulation(c_all, ada_w, ada_b):
    R = c_all.shape[0]
    D = D_MODEL
    out = pl.pallas_call(
        _mod_kernel,
        grid=(DEPTH, 6),
        in_specs=[
            pl.BlockSpec((R, D), lambda l, j: (0, 0)),
            pl.BlockSpec((None, D, D), lambda l, j: (l, 0, j)),
            pl.BlockSpec((None, 1, D), lambda l, j: (l, 0, j)),
        ],
        out_specs=pl.BlockSpec((None, R, D), lambda l, j: (l, 0, j)),
        out_shape=jax.ShapeDtypeStruct((DEPTH, R, 6 * D), F32),
        compiler_params=_params("arbitrary", "arbitrary"),
        name="adaln_mod",
    )(c_all, ada_w, ada_b.reshape(DEPTH, 1, 6 * D))
    return out.reshape(DEPTH, R, 6, D)


def _rope_tables(S):
    inv_freq = jnp.power(ROPE_THETA, -jnp.arange(ROPE_HALF, dtype=F32) / ROPE_HALF)
    ang = jnp.arange(S).astype(F32)[:, None] * inv_freq[None, :]
    cos, sin = jnp.cos(ang), jnp.sin(ang)
    ones = jnp.ones((S, ATT_HEAD_DIM - ROPE_DIM), F32)
    zeros = jnp.zeros((S, ATT_HEAD_DIM - ROPE_DIM), F32)
    zh = jnp.zeros((S, ROPE_HALF), F32)
    cos_t = jnp.concatenate([cos, cos, ones], axis=1)
    sin_up = jnp.concatenate([-sin, zh, zeros], axis=1)
    sin_dn = jnp.concatenate([zh, sin, zeros], axis=1)
    rep = LANES // ATT_HEAD_DIM
    natural = tuple(jnp.tile(t, (1, rep)) for t in (cos_t, sin_up, sin_dn))
    tm = min(ROW_TILE, S)
    tables = []
    for _, dil in DILATED_PATTERNS:
        tables += [t.reshape(S // tm, tm // dil, dil, LANES).transpose(0, 2, 1, 3).reshape(S, LANES)
                   for t in natural]
    return tables


def _ab_in_kernel(x_ref, mod_ref, gain_ref, w_ref, *rest, tm):
    tab_refs = rest[:3 * N_PATTERNS]
    a_ref = rest[3 * N_PATTERNS]
    out_refs = rest[3 * N_PATTERNS + 1:3 * N_PATTERNS + 1 + N_PATTERNS]
    h_ref = rest[-1]
    hf = _norm_mod(x_ref[...], gain_ref[...], mod_ref[1:2, :], mod_ref[0:1, :])
    n_slab = hf.shape[1] // LANES
    for c in range(n_slab):
        h_ref[c] = hf[:, c * LANES:(c + 1) * LANES]
    h = hf.astype(BF16)

    def seg(lhs, j):
        return jnp.dot(lhs, w_ref[:, j * ATT_WIDTH:(j + 1) * ATT_WIDTH], preferred_element_type=F32)

    a_ref[...] = (seg(h, 0) * jax.nn.sigmoid(seg(h, 1))).astype(a_ref.dtype)

    def rope(p, tabs, scale):
        cos_t, sin_up, sin_dn = tabs
        parts = []
        for j in range(ATT_WIDTH // LANES):
            t = p[:, j * LANES:(j + 1) * LANES]
            up = pltpu.roll(t, LANES - ROPE_HALF, 1)
            dn = pltpu.roll(t, ROPE_HALF, 1)
            r = t * cos_t + up * sin_up + dn * sin_dn
            if scale != 1.0:
                r = r * scale
            parts.append(r.astype(BF16))
        return jnp.concatenate(parts, axis=1)

    for g, (_, dil) in enumerate(DILATED_PATTERNS):
        lt = tm // dil
        if dil == 1:
            hg = h
        else:
            hg = jnp.concatenate(
                [jnp.concatenate([h_ref[c, pl.ds(r, lt, stride=dil), :] for c in range(n_slab)], axis=1)
                 for r in range(dil)], axis=0).astype(BF16)
        tabs = tuple(t[...] for t in tab_refs[3 * g:3 * g + 3])
        q = rope(seg(hg, 2 + 3 * g), tabs, ATT_HEAD_DIM ** -0.5)
        k = rope(seg(hg, 3 + 3 * g), tabs, 1.0)
        v = seg(hg, 4 + 3 * g).astype(BF16)
        for c, val in enumerate((q, k, v)):
            for r in range(dil):
                out_refs[g][r, :, c * ATT_WIDTH:(c + 1) * ATT_WIDTH] = val[r * lt:(r + 1) * lt, :]


def _ab_in_proj(x, mod, n_off, gain, w_in, tables):
    N, S, D = x.shape
    tm = min(ROW_TILE, S)
    tab_spec = pl.BlockSpec((tm, LANES), lambda n, i: (i, 0))
    out_specs = [pl.BlockSpec((None, tm, CONV_CH), lambda n, i: (n, i, 0))]
    out_shape = [jax.ShapeDtypeStruct((N, S, CONV_CH), BF16)]
    for _, dil in DILATED_PATTERNS:
        out_specs.append(pl.BlockSpec((None, dil, tm // dil, 3 * ATT_WIDTH), lambda n, i: (n, 0, i, 0)))
        out_shape.append(jax.ShapeDtypeStruct((N, dil, S // dil, 3 * ATT_WIDTH), BF16))
    return pl.pallas_call(
        functools.partial(_ab_in_kernel, tm=tm),
        grid=(N, S // tm),
        in_specs=[
            pl.BlockSpec((None, tm, D), lambda n, i: (n, i, 0)),
            pl.BlockSpec((None, None, 6, D), lambda n, i: (0, n + n_off, 0, 0)),
            pl.BlockSpec((1, D), lambda n, i: (0, 0)),
            pl.BlockSpec((D, AB_IN), lambda n, i: (0, 0)),
        ] + [tab_spec] * (3 * N_PATTERNS),
        out_specs=out_specs,
        out_shape=out_shape,
        scratch_shapes=[pltpu.VMEM((D // LANES, tm, LANES), F32)],
        compiler_params=_params("parallel", "parallel"),
        name="ab_in_proj",
    )(x, mod, gain, w_in, *tables)


def _attn_kernel(q_ref, kp_ref, kc_ref, kn_ref, vp_ref, vc_ref, vn_ref, o_ref, l_ref, *, tq, n_blocks):
    i = pl.program_id(2)
    tk = tq + 2 * ATT_RADIUS
    row = lax.broadcasted_iota(jnp.int32, (tq, tk), 0)
    col = lax.broadcasted_iota(jnp.int32, (tq, tk), 1)
    rel = col - row
    lo = jnp.where(i == 0, ATT_RADIUS, 0)
    hi = jnp.where(i == n_blocks - 1, tq + ATT_RADIUS, tk)
    mask = (rel >= 0) & (rel <= 2 * ATT_RADIUS) & (col >= lo) & (col < hi)
    lane = lax.broadcasted_iota(jnp.int32, (1, LANES), 1)
    first_head = lane < ATT_HEAD_DIM
    lse_all = jnp.zeros((tq, LANES), F32)

    for j in range(ATT_WIDTH // LANES):
        sl = slice(j * LANES, (j + 1) * LANES)
        q = q_ref[:, sl]
        kw = jnp.concatenate([kp_ref[:, sl], kc_ref[:, sl], kn_ref[:, sl]], axis=0)
        vw = jnp.concatenate([vp_ref[:, sl], vc_ref[:, sl], vn_ref[:, sl]], axis=0)
        o_pair = None
        for hh in range(2):
            sel = first_head if hh == 0 else jnp.logical_not(first_head)
            qm = jnp.where(sel, q, jnp.zeros_like(q))
            vm = jnp.where(sel, vw, jnp.zeros_like(vw))
            s = lax.dot_general(qm, kw, (((1,), (1,)), ((), ())), preferred_element_type=F32)
            s = jnp.where(mask, s, NEG_INF)
            m = jnp.max(s, axis=-1, keepdims=True)
            p = jnp.exp(s - m)
            l = jnp.sum(p, axis=-1, keepdims=True)
            acc = jnp.dot(p.astype(vm.dtype), vm, preferred_element_type=F32)
            o_h = acc / l
            o_pair = o_h if hh == 0 else o_pair + o_h
            lse_all = jnp.where(lane == 2 * j + hh, m + jnp.log(l), lse_all)
        o_ref[:, sl] = o_pair.astype(o_ref.dtype)
    l_ref[...] = lse_all


def _dilated_attention(qkv_g, g):
    N, dil, L, _ = qkv_g.shape
    tq = min(ATT_TQ, L)
    nb = L // tq
    hb = tq // ATT_RADIUS
    n_halo = L // ATT_RADIUS

    def cur(c):
        return pl.BlockSpec((None, None, tq, ATT_WIDTH), lambda n, r, i: (n, r, i, c))

    def before(c):
        return pl.BlockSpec((None, None, ATT_RADIUS, ATT_WIDTH),
                            lambda n, r, i: (n, r, jnp.maximum(i * hb - 1, 0), c))

    def after(c):
        return pl.BlockSpec((None, None, ATT_RADIUS, ATT_WIDTH),
                            lambda n, r, i: (n, r, jnp.minimum((i + 1) * hb, n_halo - 1), c))

    return pl.pallas_call(
        functools.partial(_attn_kernel, tq=tq, n_blocks=nb),
        grid=(N, dil, nb),
        in_specs=[cur(0), before(1), cur(1), after(1), before(2), cur(2), after(2)],
        out_specs=[
            pl.BlockSpec((None, None, tq, ATT_WIDTH), lambda n, r, i: (n, r, i, 0)),
            pl.BlockSpec((None, None, tq, LANES), lambda n, r, i: (n, r, i, 0)),
        ],
        out_shape=[
            jax.ShapeDtypeStruct((N, dil, L, ATT_WIDTH), BF16),
            jax.ShapeDtypeStruct((N, dil, L, LANES), F32),
        ],
        compiler_params=_params("parallel", "parallel", "parallel"),
        name=f"dilated_attn_{g}",
    )(*([qkv_g] * 7))


def _ab_out_kernel(ap_ref, ac_ref, an_ref, o0_ref, l0_ref, o1_ref, l1_ref, o2_ref, l2_ref, x_ref, mod_ref,
                   cw_ref, cb_ref, lg_ref, lb_ref, w_ref, o_ref, sh_ref, act_ref, on_ref, ln_ref, *, tm, n_tiles):
    i = pl.program_id(1)
    H = CONV_HALO
    SUB = 8

    sh_ref[0, 0:H, :] = jnp.where(i > 0, ap_ref[...].astype(F32), 0.0)
    sh_ref[0, H:H + tm, :] = ac_ref[...].astype(F32)
    sh_ref[0, H + tm:H + tm + H, :] = jnp.where(i < n_tiles - 1, an_ref[...].astype(F32), 0.0)
    span = tm + 2 * H - SUB
    for s in range(1, SUB):
        sh_ref[s, 0:span, :] = sh_ref[0, s:s + span, :]

    off = H - CONV_WIDTH // 2

    def chunk(c, carry):
        r0 = pl.multiple_of(c * CONV_ROWS, CONV_ROWS)
        acc = jnp.broadcast_to(cb_ref[...], (CONV_ROWS, CONV_CH))
        for j in range(CONV_WIDTH):
            s, blk = (off + j) % SUB, (off + j) // SUB
            acc = acc + cw_ref[j:j + 1, :] * sh_ref[s, pl.ds(r0 + SUB * blk, CONV_ROWS), :]
        mu = jnp.mean(acc, axis=-1, keepdims=True)
        xc = acc - mu
        y = xc * lax.rsqrt(jnp.mean(xc * xc, axis=-1, keepdims=True) + EPS)
        y = y * lg_ref[...] + lb_ref[...]
        act_ref[pl.ds(r0, CONV_ROWS), :] = (y * jax.nn.sigmoid(y)).astype(act_ref.dtype)
        return carry

    lax.fori_loop(0, tm // CONV_ROWS, chunk, 0)

    for g, (o_g, l_g) in enumerate(((o0_ref, l0_ref), (o1_ref, l1_ref), (o2_ref, l2_ref))):
        dil = DILATED_PATTERNS[g][1]
        lt = tm // dil
        for r in range(dil):
            rows = pl.ds(r, lt, stride=dil) if dil > 1 else pl.ds(0, tm)
            o_r = o_g[r].astype(F32)
            for c in range(ATT_WIDTH // LANES):
                on_ref[g, c, rows, :] = o_r[:, c * LANES:(c + 1) * LANES]
            ln_ref[g, rows, :] = l_g[r]
    lses = [ln_ref[g] for g in range(N_PATTERNS)]
    mx = functools.reduce(jnp.maximum, lses)
    es = [jnp.exp(l - mx) for l in lses]
    tot = functools.reduce(lambda a, b: a + b, es)
    erow = lax.broadcasted_iota(jnp.int32, (LANES, ATT_WIDTH), 0)
    ecol = lax.broadcasted_iota(jnp.int32, (LANES, ATT_WIDTH), 1)
    expand = (ecol // ATT_HEAD_DIM == erow).astype(BF16)
    att = None
    for g in range(N_PATTERNS):
        wgt = es[g] / tot
        hi = wgt.astype(BF16)
        lo = (wgt - hi.astype(F32)).astype(BF16)
        wide = jnp.dot(hi, expand, preferred_element_type=F32) + jnp.dot(lo, expand, preferred_element_type=F32)
        term = wide * jnp.concatenate([on_ref[g, c] for c in range(ATT_WIDTH // LANES)], axis=1)
        att = term if att is None else att + term

    mix = jnp.dot(act_ref[...], w_ref[0:CONV_CH, :], preferred_element_type=F32)
    mix = mix + jnp.dot(att.astype(BF16), w_ref[CONV_CH:, :], preferred_element_type=F32)
    o_ref[...] = x_ref[...] + mod_ref[2:3, :] * mix


def _ab_out_proj(a, att_parts, x, mod, n_off, conv_w, conv_b, ln_g, ln_b, w_out):
    N, S, D = x.shape
    tm = min(ROW_TILE, S)
    nt = S // tm
    hb = tm // CONV_HALO
    n_halo = S // CONV_HALO
    row = lambda n, i: (n, i, 0)
    const = lambda n, i: (0, 0)
    att_specs, att_args = [], []
    for (_, dil), (o_g, l_g) in zip(DILATED_PATTERNS, att_parts):
        att_specs.append(pl.BlockSpec((None, dil, tm // dil, ATT_WIDTH), lambda n, i: (n, 0, i, 0)))
        att_specs.append(pl.BlockSpec((None, dil, tm // dil, LANES), lambda n, i: (n, 0, i, 0)))
        att_args += [o_g, l_g]
    return pl.pallas_call(
        functools.partial(_ab_out_kernel, tm=tm, n_tiles=nt),
        grid=(N, nt),
        in_specs=[
            pl.BlockSpec((None, CONV_HALO, CONV_CH), lambda n, i: (n, jnp.maximum(i * hb - 1, 0), 0)),
            pl.BlockSpec((None, tm, CONV_CH), row),
            pl.BlockSpec((None, CONV_HALO, CONV_CH), lambda n, i: (n, jnp.minimum((i + 1) * hb, n_halo - 1), 0)),
        ] + att_specs + [
            pl.BlockSpec((None, tm, D), row),
            pl.BlockSpec((None, None, 6, D), lambda n, i: (0, n + n_off, 0, 0)),
            pl.BlockSpec((CONV_WIDTH, CONV_CH), const),
            pl.BlockSpec((1, CONV_CH), const),
            pl.BlockSpec((1, CONV_CH), const),
            pl.BlockSpec((1, CONV_CH), const),
            pl.BlockSpec((D, D), const),
        ],
        out_specs=pl.BlockSpec((None, tm, D), row),
        out_shape=jax.ShapeDtypeStruct((N, S, D), F32),
        scratch_shapes=[
            pltpu.VMEM((8, tm + 2 * CONV_HALO, CONV_CH), F32),
            pltpu.VMEM((tm, CONV_CH), BF16),
            pltpu.VMEM((N_PATTERNS, ATT_WIDTH // LANES, tm, LANES), F32),
            pltpu.VMEM((N_PATTERNS, tm, LANES), F32),
        ],
        compiler_params=_params("parallel", "parallel"),
        name="ab_out_proj",
    )(a, a, a, *att_args, x, mod, conv_w, conv_b, ln_g, ln_b, w_out)


def _mlp_kernel(x_ref, mod_ref, gain_ref, w1_ref, w2_ref, *rest, final):
    if final:
        fg_ref, o_ref = rest
    else:
        (o_ref,) = rest
    x = x_ref[...]
    h = _norm_mod(x, gain_ref[...], mod_ref[4:5, :], mod_ref[3:4, :]).astype(BF16)
    acc = jnp.zeros(x.shape, F32)
    for c in range(D_FF // MLP_FF_TILE):
        sl = slice(c * MLP_FF_TILE, (c + 1) * MLP_FF_TILE)
        u = jnp.maximum(jnp.dot(h, w1_ref[:, sl], preferred_element_type=F32), 0.0)
        acc = acc + jnp.dot((u * u).astype(BF16), w2_ref[sl, :], preferred_element_type=F32)
    y = x + mod_ref[5:6, :] * acc
    if final:
        y = y * lax.rsqrt(jnp.mean(y * y, axis=-1, keepdims=True) + EPS) * fg_ref[...]
    o_ref[...] = y


def _mlp(x, mod, n_off, layer, gain, w1, w2, final_gain=None):
    N, S, D = x.shape
    tm = min(ROW_TILE, S)
    row = lambda n, i: (n, i, 0)
    const = lambda n, i: (0, 0)
    final = final_gain is not None
    in_specs = [
        pl.BlockSpec((None, tm, D), row),
        pl.BlockSpec((None, None, 6, D), lambda n, i: (layer, n + n_off, 0, 0)),
        pl.BlockSpec((1, D), const),
        pl.BlockSpec((D, D_FF), const),
        pl.BlockSpec((D_FF, D), const),
    ]
    args = [x, mod, gain, w1, w2]
    if final:
        in_specs.append(pl.BlockSpec((1, D), const))
        args.append(final_gain)
    return pl.pallas_call(
        functools.partial(_mlp_kernel, final=final),
        grid=(N, S // tm),
        in_specs=in_specs,
        out_specs=pl.BlockSpec((None, tm, D), row),
        out_shape=jax.ShapeDtypeStruct((N, S, D), F32),
        compiler_params=_params("parallel", "parallel"),
        name=f"mlp_{layer}",
    )(*args)


def _c_in_kernel(x_ref, mod_ref, gain_ref, w_ref, gb_ref, qkv_ref, og_ref, gate_ref):
    W = MLSTM_WIDTH
    x = x_ref[...]
    h = _norm_mod(x, gain_ref[...], mod_ref[1:2, :], mod_ref[0:1, :]).astype(BF16)

    def seg(lo, width):
        return jnp.dot(h, w_ref[:, lo:lo + width], preferred_element_type=F32)

    qkv_ref[:, 0:W] = seg(0, W).astype(qkv_ref.dtype)
    qkv_ref[:, W:2 * W] = (seg(W, W) * (MLSTM_HEAD_DIM ** -0.5)).astype(qkv_ref.dtype)
    qkv_ref[:, 2 * W:3 * W] = seg(2 * W, W).astype(qkv_ref.dtype)
    og_ref[...] = jax.nn.sigmoid(seg(3 * W, W)).astype(og_ref.dtype)
    z = seg(4 * W, GATE_PAD) + gb_ref[...]
    lane = lax.broadcasted_iota(jnp.int32, (1, GATE_PAD), 1)
    is_forget = (lane % (2 * MLSTM_HEADS)) >= MLSTM_HEADS
    log_sig = jnp.minimum(z, 0.0) - jnp.log1p(jnp.exp(-jnp.abs(z)))
    gate_ref[...] = jnp.where(is_forget, log_sig, z)


def _c_in_proj(x, mod, n_off, gain, w_in, gate_b):
    N, S, D = x.shape
    W = MLSTM_WIDTH
    tm = min(ROW_TILE, S)
    row = lambda n, i: (n, i, 0)
    const = lambda n, i: (0, 0)
    return pl.pallas_call(
        _c_in_kernel,
        grid=(N, S // tm),
        in_specs=[
            pl.BlockSpec((None, tm, D), row),
            pl.BlockSpec((None, None, 6, D), lambda n, i: (1, n + n_off, 0, 0)),
            pl.BlockSpec((1, D), const),
            pl.BlockSpec((D, 4 * W + GATE_PAD), const),
            pl.BlockSpec((1, GATE_PAD), const),
        ],
        out_specs=[
            pl.BlockSpec((None, tm, 3 * W), row),
            pl.BlockSpec((None, tm, W), row),
            pl.BlockSpec((None, tm, GATE_PAD), row),
        ],
        out_shape=[
            jax.ShapeDtypeStruct((N, S, 3 * W), BF16),
            jax.ShapeDtypeStruct((N, S, W), BF16),
            jax.ShapeDtypeStruct((N, S, GATE_PAD), F32),
        ],
        compiler_params=_params("parallel", "parallel"),
        name="c_in_proj",
    )(x, mod, gain, w_in, gate_b)


def _split3(x):
    hi = x.astype(BF16)
    r1 = x - hi.astype(F32)
    mid = r1.astype(BF16)
    lo = (r1 - mid.astype(F32)).astype(BF16)
    return hi, mid, lo


def _mlstm_kernel(qf_ref, kf_ref, vf_ref, gf_ref, qb_ref, kb_ref, vb_ref, gb_ref,
                  hf_ref, hb_ref, c_ref, n_ref, m_ref):
    L = MLSTM_CHUNK
    dh = MLSTM_HEAD_DIM

    @pl.when(pl.program_id(1) == 0)
    def _():
        c_ref[...] = jnp.zeros_like(c_ref)
        n_ref[...] = jnp.zeros_like(n_ref)
        m_ref[...] = jnp.zeros_like(m_ref)

    row = lax.broadcasted_iota(jnp.int32, (L, L), 0)
    col = lax.broadcasted_iota(jnp.int32, (L, L), 1)

    dirs = (
        (qf_ref, kf_ref, vf_ref, gf_ref, hf_ref, row >= col, L - 1),
        (qb_ref, kb_ref, vb_ref, gb_ref, hb_ref, row <= col, 0),
    )
    for d, (q_ref, k_ref, v_ref, g_ref, h_ref, allowed, last) in enumerate(dirs):
        gates = g_ref[...]
        tri = allowed.astype(BF16)
        csum = sum(jnp.dot(tri, piece, preferred_element_type=F32) for piece in _split3(gates))
        gates_t = gates.T
        csum_t = csum.T
        for hd in range(MLSTM_HEADS):
            ci = 2 * MLSTM_HEADS * d + hd
            cf = ci + MLSTM_HEADS
            sl = slice(hd * dh, (hd + 1) * dh)
            q = q_ref[:, sl]
            k = k_ref[:, sl]
            v = v_ref[:, sl]
            C = c_ref[d, hd]
            n = n_ref[d, hd]
            m = m_ref[d, hd][:, 0:1]
            b_col = csum[:, cf:cf + 1]
            b_row = csum_t[cf:cf + 1, :]
            ig_col = gates[:, ci:ci + 1]
            ig_row = gates_t[ci:ci + 1, :]
            log_d = jnp.where(allowed, b_col - b_row + ig_row, NEG_INF)
            inter = b_col + m
            mt = jnp.maximum(inter, jnp.max(log_d, axis=-1, keepdims=True))
            s = lax.dot_general(q, k, (((1,), (1,)), ((), ())), preferred_element_type=F32)
            sc = s * jnp.exp(log_d - mt)
            ei = jnp.exp(inter - mt)
            num = jnp.dot(sc.astype(BF16), v, preferred_element_type=F32)
            num = num + ei * jnp.dot(q, C.astype(BF16), preferred_element_type=F32)
            qn = jnp.sum(q.astype(F32) * n, axis=-1, keepdims=True)
            den = jnp.sum(sc, axis=-1, keepdims=True) + ei * qn
            h_ref[:, sl] = (num / jnp.maximum(jnp.abs(den), jnp.exp(-mt))).astype(h_ref.dtype)

            b_last = csum[last:last + 1, cf:cf + 1]
            m_new = jnp.maximum(b_last + m, jnp.max(b_last - b_row + ig_row, axis=-1, keepdims=True))
            decay = jnp.exp(b_last + m - m_new)
            w_col = jnp.exp(b_last - b_col + ig_col - m_new)
            kw = k.astype(F32) * w_col
            upd = lax.dot_general(kw.astype(BF16), v, (((0,), (0,)), ((), ())), preferred_element_type=F32)
            c_ref[d, hd] = decay * C + upd
            n_ref[d, hd] = decay * n + jnp.sum(kw, axis=0, keepdims=True)
            m_ref[d, hd] = jnp.broadcast_to(m_new, (1, LANES))


def _mlstm_scan(qkv, gates):
    N, S, _ = qkv.shape
    W = MLSTM_WIDTH
    L = MLSTM_CHUNK
    nc = S // L
    fwd = lambda c: (lambda n, i: (n, i, c))
    bwd = lambda c: (lambda n, i: (n, nc - 1 - i, c))
    blk = lambda im: pl.BlockSpec((None, L, W), im)
    gblk = lambda im: pl.BlockSpec((None, L, GATE_PAD), im)
    return pl.pallas_call(
        _mlstm_kernel,
        grid=(N, nc),
        in_specs=[blk(fwd(0)), blk(fwd(1)), blk(fwd(2)), gblk(fwd(0)),
                  blk(bwd(0)), blk(bwd(1)), blk(bwd(2)), gblk(bwd(0))],
        out_specs=[blk(fwd(0)), blk(bwd(0))],
        out_shape=[jax.ShapeDtypeStruct((N, S, W), F32)] * 2,
        scratch_shapes=[
            pltpu.VMEM((2, MLSTM_HEADS, MLSTM_HEAD_DIM, MLSTM_HEAD_DIM), F32),
            pltpu.VMEM((2, MLSTM_HEADS, 1, MLSTM_HEAD_DIM), F32),
            pltpu.VMEM((2, MLSTM_HEADS, 1, LANES), F32),
        ],
        compiler_params=_params("parallel", "arbitrary"),
        name="mlstm_scan",
    )(qkv, qkv, qkv, gates, qkv, qkv, qkv, gates)


def _c_out_kernel(hf_ref, hb_ref, og_ref, x_ref, mod_ref, hn_ref, w_ref, o_ref):
    dh = MLSTM_HEAD_DIM
    parts = []
    for hd in range(MLSTM_HEADS):
        sl = slice(hd * dh, (hd + 1) * dh)
        ht = hf_ref[:, sl].astype(F32) + hb_ref[:, sl].astype(F32)
        ht = ht * lax.rsqrt(jnp.mean(ht * ht, axis=-1, keepdims=True) + EPS)
        parts.append((og_ref[:, sl].astype(F32) * (ht * hn_ref[:, sl])).astype(BF16))
    z = jnp.concatenate(parts, axis=1)
    mix = jnp.dot(z, w_ref[...], preferred_element_type=F32)
    o_ref[...] = x_ref[...] + mod_ref[2:3, :] * mix


def _c_out_proj(hf, hb, og, x, mod, n_off, head_norm, w_out):
    N, S, D = x.shape
    W = MLSTM_WIDTH
    tm = min(ROW_TILE, S)
    row = lambda n, i: (n, i, 0)
    const = lambda n, i: (0, 0)
    return pl.pallas_call(
        _c_out_kernel,
        grid=(N, S // tm),
        in_specs=[
            pl.BlockSpec((None, tm, W), row),
            pl.BlockSpec((None, tm, W), row),
            pl.BlockSpec((None, tm, W), row),
            pl.BlockSpec((None, tm, D), row),
            pl.BlockSpec((None, None, 6, D), lambda n, i: (1, n + n_off, 0, 0)),
            pl.BlockSpec((1, W), const),
            pl.BlockSpec((W, D), const),
        ],
        out_specs=pl.BlockSpec((None, tm, D), row),
        out_shape=jax.ShapeDtypeStruct((N, S, D), F32),
        compiler_params=_params("parallel", "parallel"),
        name="c_out_proj",
    )(hf, hb, og, x, mod, head_norm, w_out)


def _trunk(x, mod, n_off, p):
    N, S, D = x.shape
    tables = _rope_tables(S)
    a, *qkv = _ab_in_proj(x, mod, n_off, p["norm_mix0"], p["ab_w_in"], tables)
    att_parts = [_dilated_attention(qkv[g], g) for g in range(N_PATTERNS)]
    x = _ab_out_proj(a, att_parts, x, mod, n_off, p["conv_w"], p["conv_b"], p["conv_ln_g"],
                     p["conv_ln_b"], p["ab_w_out"])
    x = _mlp(x, mod, n_off, 0, p["norm_mlp0"], p["mlp_w1_0"], p["mlp_w2_0"])
    qkv1, og, gates = _c_in_proj(x, mod, n_off, p["norm_mix1"], p["c_w_in"], p["c_gate_b"])
    hf, hb = _mlstm_scan(qkv1, gates)
    x = _c_out_proj(hf, hb, og, x, mod, n_off, p["c_head_norm"], p["c_w_out"])
    return _mlp(x, mod, n_off, 1, p["norm_mlp1"], p["mlp_w1_1"], p["mlp_w2_1"], p["norm_final"])


def kernel(x_prompt, x_sample, c_prompt, c_sample, ada_w, ada_b, norm_mix, norm_mlp, ab_w_in, conv_w, conv_b, conv_ln_g, conv_ln_b, ab_w_out, c_w_in, c_gate_b, c_head_norm, c_w_out, mlp_w1, mlp_w2, norm_final):
    D = D_MODEL
    n_p, n_s = c_prompt.shape[0], c_sample.shape[0]
    rows = n_p + n_s
    rows_pad = -(-rows // 8) * 8
    c_all = jnp.concatenate([c_prompt, c_sample, jnp.zeros((rows_pad - rows, D), F32)], axis=0)
    mod = _modulation(c_all, ada_w, ada_b)

    W = MLSTM_WIDTH
    pad = GATE_PAD - N_GATES
    p = {
        "norm_mix0": norm_mix[0].reshape(1, D),
        "norm_mix1": norm_mix[1].reshape(1, D),
        "norm_mlp0": norm_mlp[0].reshape(1, D),
        "norm_mlp1": norm_mlp[1].reshape(1, D),
        "ab_w_in": ab_w_in[0].astype(BF16),
        "conv_w": conv_w[0],
        "conv_b": conv_b[0].reshape(1, CONV_CH),
        "conv_ln_g": conv_ln_g[0].reshape(1, CONV_CH),
        "conv_ln_b": conv_ln_b[0].reshape(1, CONV_CH),
        "ab_w_out": ab_w_out[0].astype(BF16),
        "c_w_in": jnp.pad(c_w_in[0], ((0, 0), (0, pad))).astype(BF16),
        "c_gate_b": jnp.pad(c_gate_b[0], (0, pad)).reshape(1, GATE_PAD),
        "c_head_norm": c_head_norm[0].reshape(1, W),
        "c_w_out": c_w_out[0].astype(BF16),
        "mlp_w1_0": mlp_w1[0].astype(BF16),
        "mlp_w2_0": mlp_w2[0].astype(BF16),
        "mlp_w1_1": mlp_w1[1].astype(BF16),
        "mlp_w2_1": mlp_w2[1].astype(BF16),
        "norm_final": norm_final.reshape(1, D),
    }
    y_prompt = _trunk(x_prompt, mod, 0, p)
    y_sample = _trunk(x_sample, mod, n_p, p)
    return (y_prompt, y_sample)
```

```python
import functools

import jax
import jax.numpy as jnp
from jax import lax
from jax.experimental import pallas as pl
from jax.experimental.pallas import tpu as pltpu

D_MODEL = 1024
DEPTH = 2
CONV_CH = 512
CONV_WIDTH = 31
CONV_HALO = 16
ATT_HEADS = 8
ATT_HEAD_DIM = 64
ATT_WIDTH = ATT_HEADS * ATT_HEAD_DIM
DILATED_PATTERNS = ((128, 1), (512, 4), (2048, 16))
N_PATTERNS = len(DILATED_PATTERNS)
ATT_RADIUS = 64
ROPE_THETA = 500000.0
ROPE_DIM = ATT_HEAD_DIM // 4
ROPE_HALF = ROPE_DIM // 2
AB_IN = 2 * CONV_CH + N_PATTERNS * 3 * ATT_WIDTH
QKV_W = N_PATTERNS * 3 * ATT_WIDTH
MLSTM_HEADS = 8
MLSTM_WIDTH = D_MODEL
MLSTM_HEAD_DIM = MLSTM_WIDTH // MLSTM_HEADS
MLSTM_CHUNK = 128
N_GATES = 4 * MLSTM_HEADS
GATE_PAD = 128
D_FF = 4 * D_MODEL
EPS = 1e-6
NEG_INF = -1e30

LANES = 128
VMEM_LIMIT = 56 * 1024 * 1024
ROW_TILE = 512
ATT_TQ = 128
ATT_TILES_PER_STEP = 4
CONV_ROWS = 32
CONV_CHAINS = 4
NORM_ROWS = 128
MLP_FF_TILE = 1024

F32 = jnp.float32
BF16 = jnp.bfloat16


def _params(*sem):
    return pltpu.CompilerParams(dimension_semantics=sem, vmem_limit_bytes=VMEM_LIMIT)


def _norm_mod(x, gain, scale, shift):
    y = x * lax.rsqrt(jnp.mean(x * x, axis=-1, keepdims=True) + EPS)
    return (y * gain) * (1.0 + scale) + shift


def _mod_kernel(c_ref, w_ref, b_ref, o_ref):
    c = c_ref[...]
    s = c * jax.nn.sigmoid(c)
    o_ref[...] = jnp.dot(s, w_ref[...], preferred_element_type=F32,
                         precision=lax.Precision.HIGHEST) + b_ref[...]


def _modulation(c_all, ada_w, ada_b):
    R = c_all.shape[0]
    D = D_MODEL
    out = pl.pallas_call(
        _mod_kernel,
        grid=(DEPTH, 6),
        in_specs=[
            pl.BlockSpec((R, D), lambda l, j: (0, 0)),
            pl.BlockSpec((None, D, D), lambda l, j: (l, 0, j)),
            pl.BlockSpec((None, 1, D), lambda l, j: (l, 0, j)),
        ],
        out_specs=pl.BlockSpec((None, R, D), lambda l, j: (l, 0, j)),
        out_shape=jax.ShapeDtypeStruct((DEPTH, R, 6 * D), F32),
        compiler_params=_params("arbitrary", "arbitrary"),
        name="adaln_mod",
    )(c_all, ada_w, ada_b.reshape(DEPTH, 1, 6 * D))
    return out.reshape(DEPTH, R, 6, D)


def _rope_tables(S):
    inv_freq = jnp.power(ROPE_THETA, -jnp.arange(ROPE_HALF, dtype=F32) / ROPE_HALF)
    ang = jnp.arange(S).astype(F32)[:, None] * inv_freq[None, :]
    cos, sin = jnp.cos(ang), jnp.sin(ang)
    ones = jnp.ones((S, ATT_HEAD_DIM - ROPE_DIM), F32)
    zeros = jnp.zeros((S, ATT_HEAD_DIM - ROPE_DIM), F32)
    zh = jnp.zeros((S, ROPE_HALF), F32)
    cos_t = jnp.concatenate([cos, cos, ones], axis=1)
    sin_up = jnp.concatenate([-sin, zh, zeros], axis=1)
    sin_dn = jnp.concatenate([zh, sin, zeros], axis=1)
    rep = LANES // ATT_HEAD_DIM
    return tuple(jnp.tile(t, (1, rep)) for t in (cos_t, sin_up, sin_dn))


def _ab_in_kernel(x_ref, mod_ref, gain_ref, w_ref, *rest, tm):
    tab_refs = rest[:3]
    a_ref = rest[3]
    out_refs = rest[4:4 + N_PATTERNS]
    h_ref = rest[-1]
    hf = _norm_mod(x_ref[...], gain_ref[...], mod_ref[1:2, :], mod_ref[0:1, :])
    n_slab = hf.shape[1] // LANES
    for c in range(n_slab):
        h_ref[c] = hf[:, c * LANES:(c + 1) * LANES]
    h = hf.astype(BF16)

    def seg(lhs, j):
        return jnp.dot(lhs, w_ref[:, j * ATT_WIDTH:(j + 1) * ATT_WIDTH], preferred_element_type=F32)

    a_ref[...] = (seg(h, 0) * jax.nn.sigmoid(seg(h, 1))).astype(a_ref.dtype)

    def rope(p, tabs, scale):
        cos_t, sin_up, sin_dn = tabs
        parts = []
        for j in range(ATT_WIDTH // LANES):
            t = p[:, j * LANES:(j + 1) * LANES]
            up = pltpu.roll(t, LANES - ROPE_HALF, 1)
            dn = pltpu.roll(t, ROPE_HALF, 1)
            r = t * cos_t + up * sin_up + dn * sin_dn
            if scale != 1.0:
                r = r * scale
            parts.append(r.astype(BF16))
        return jnp.concatenate(parts, axis=1)

    for g, (_, dil) in enumerate(DILATED_PATTERNS):
        lt = tm // dil
        if dil == 1:
            hg = h
            tabs = tuple(t[...] for t in tab_refs)
        else:
            hg = jnp.concatenate(
                [jnp.concatenate([h_ref[c, pl.ds(r, lt, stride=dil), :] for c in range(n_slab)], axis=1)
                 for r in range(dil)], axis=0).astype(BF16)
            tabs = tuple(jnp.concatenate([t[pl.ds(r, lt, stride=dil), :] for r in range(dil)], axis=0)
                         for t in tab_refs)
        q = rope(seg(hg, 2 + 3 * g), tabs, ATT_HEAD_DIM ** -0.5)
        k = rope(seg(hg, 3 + 3 * g), tabs, 1.0)
        v = seg(hg, 4 + 3 * g).astype(BF16)
        for c, val in enumerate((q, k, v)):
            for r in range(dil):
                out_refs[g][r, :, c * ATT_WIDTH:(c + 1) * ATT_WIDTH] = val[r * lt:(r + 1) * lt, :]


def _ab_in_proj(x, mod, n_off, gain, w_in, tables):
    N, S, D = x.shape
    tm = min(ROW_TILE, S)
    tab_spec = pl.BlockSpec((tm, LANES), lambda n, i: (i, 0))
    out_specs = [pl.BlockSpec((None, tm, CONV_CH), lambda n, i: (n, i, 0))]
    out_shape = [jax.ShapeDtypeStruct((N, S, CONV_CH), BF16)]
    for _, dil in DILATED_PATTERNS:
        out_specs.append(pl.BlockSpec((None, dil, tm // dil, 3 * ATT_WIDTH), lambda n, i: (n, 0, i, 0)))
        out_shape.append(jax.ShapeDtypeStruct((N, dil, S // dil, 3 * ATT_WIDTH), BF16))
    return pl.pallas_call(
        functools.partial(_ab_in_kernel, tm=tm),
        grid=(N, S // tm),
        in_specs=[
            pl.BlockSpec((None, tm, D), lambda n, i: (n, i, 0)),
            pl.BlockSpec((None, None, 6, D), lambda n, i: (0, n + n_off, 0, 0)),
            pl.BlockSpec((1, D), lambda n, i: (0, 0)),
            pl.BlockSpec((D, AB_IN), lambda n, i: (0, 0)),
        ] + [tab_spec] * 3,
        out_specs=out_specs,
        out_shape=out_shape,
        scratch_shapes=[pltpu.VMEM((D // LANES, tm, LANES), F32)],
        compiler_params=_params("parallel", "parallel"),
        name="ab_in_proj",
    )(x, mod, gain, w_in, *tables)


def _attn_kernel(q_ref, kp_ref, kc_ref, kn_ref, vp_ref, vc_ref, vn_ref, o_ref, l_ref, *, tq, n_sub, n_seq, n_blocks):
    i = pl.program_id(2)
    R = ATT_RADIUS
    bq = n_sub * tq
    tk = tq + 2 * R
    row = lax.broadcasted_iota(jnp.int32, (tq, tk), 0)
    col = lax.broadcasted_iota(jnp.int32, (tq, tk), 1)
    rel = col - row
    band = (rel >= 0) & (rel <= 2 * R)
    masks = []
    for t in range(n_sub):
        mask = band
        if t == 0:
            mask = mask & (col >= jnp.where(i == 0, R, 0))
        if t == n_sub - 1:
            mask = mask & (col < jnp.where(i == n_blocks - 1, tq + R, tk))
        masks.append(mask)
    lane = lax.broadcasted_iota(jnp.int32, (1, LANES), 1)
    first_head = lane < ATT_HEAD_DIM

    def window(p_ref, c_ref, n_ref, r, t, sl):
        lo, hi = t * tq - R, (t + 1) * tq + R
        parts = []
        if lo < 0:
            parts.append(p_ref[r, :, sl])
        parts.append(c_ref[r, max(lo, 0):min(hi, bq), sl])
        if hi > bq:
            parts.append(n_ref[r, :, sl])
        return jnp.concatenate(parts, axis=0)

    for r in range(n_seq):
        for t in range(n_sub):
            rows = slice(t * tq, (t + 1) * tq)
            lse_all = jnp.zeros((tq, LANES), F32)
            for j in range(ATT_WIDTH // LANES):
                sl = slice(j * LANES, (j + 1) * LANES)
                q = q_ref[r, rows, sl]
                kw = window(kp_ref, kc_ref, kn_ref, r, t, sl)
                vw = window(vp_ref, vc_ref, vn_ref, r, t, sl)
                o_pair = None
                for hh in range(2):
                    sel = first_head if hh == 0 else jnp.logical_not(first_head)
                    qm = jnp.where(sel, q, jnp.zeros_like(q))
                    vm = jnp.where(sel, vw, jnp.zeros_like(vw))
                    s = lax.dot_general(qm, kw, (((1,), (1,)), ((), ())), preferred_element_type=F32)
                    s = jnp.where(masks[t], s, NEG_INF)
                    m = jnp.max(s, axis=-1, keepdims=True)
                    p = jnp.exp(s - m)
                    l = jnp.sum(p, axis=-1, keepdims=True)
                    acc = jnp.dot(p.astype(vm.dtype), vm, preferred_element_type=F32)
                    o_h = acc / l
                    o_pair = o_h if hh == 0 else o_pair + o_h
                    lse_all = jnp.where(lane == 2 * j + hh, m + jnp.log(l), lse_all)
                o_ref[r, rows, sl] = o_pair.astype(o_ref.dtype)
            l_ref[r, rows, :] = lse_all


def _dilated_attention(qkv_g, g):
    N, dil, L, _ = qkv_g.shape
    tq = min(ATT_TQ, L)
    n_sub = min(ATT_TILES_PER_STEP, L // tq)
    n_seq = min(ATT_TILES_PER_STEP // n_sub, dil)
    bq = n_sub * tq
    nb = L // bq
    hb = bq // ATT_RADIUS
    n_halo = L // ATT_RADIUS

    def cur(c):
        return pl.BlockSpec((None, n_seq, bq, ATT_WIDTH), lambda n, r, i: (n, r, i, c))

    def before(c):
        return pl.BlockSpec((None, n_seq, ATT_RADIUS, ATT_WIDTH),
                            lambda n, r, i: (n, r, jnp.maximum(i * hb - 1, 0), c))

    def after(c):
        return pl.BlockSpec((None, n_seq, ATT_RADIUS, ATT_WIDTH),
                            lambda n, r, i: (n, r, jnp.minimum((i + 1) * hb, n_halo - 1), c))

    return pl.pallas_call(
        functools.partial(_attn_kernel, tq=tq, n_sub=n_sub, n_seq=n_seq, n_blocks=nb),
        grid=(N, dil // n_seq, nb),
        in_specs=[cur(0), before(1), cur(1), after(1), before(2), cur(2), after(2)],
        out_specs=[
            pl.BlockSpec((None, n_seq, bq, ATT_WIDTH), lambda n, r, i: (n, r, i, 0)),
            pl.BlockSpec((None, n_seq, bq, LANES), lambda n, r, i: (n, r, i, 0)),
        ],
        out_shape=[
            jax.ShapeDtypeStruct((N, dil, L, ATT_WIDTH), BF16),
            jax.ShapeDtypeStruct((N, dil, L, LANES), F32),
        ],
        compiler_params=_params("parallel", "parallel", "parallel"),
        name=f"dilated_attn_{g}",
    )(*([qkv_g] * 7))


def _ab_out_kernel(ap_ref, ac_ref, an_ref, o0_ref, l0_ref, o1_ref, l1_ref, o2_ref, l2_ref, x_ref, mod_ref,
                   cw_ref, cb_ref, lg_ref, lb_ref, w_ref, o_ref, sh_ref, conv_ref, act_ref, on_ref, ln_ref,
                   *, tm, n_tiles):
    i = pl.program_id(1)
    H = CONV_HALO
    SUB = 8

    sh_ref[0, 0:H, :] = jnp.where(i > 0, ap_ref[...].astype(F32), 0.0)
    sh_ref[0, H:H + tm, :] = ac_ref[...].astype(F32)
    sh_ref[0, H + tm:H + tm + H, :] = jnp.where(i < n_tiles - 1, an_ref[...].astype(F32), 0.0)
    span = tm + 2 * H - SUB
    for s in range(1, SUB):
        sh_ref[s, 0:span, :] = sh_ref[0, s:s + span, :]

    off = H - CONV_WIDTH // 2
    for c in range(CONV_CH // LANES):
        lanes = slice(c * LANES, (c + 1) * LANES)
        taps = [jnp.broadcast_to(cw_ref[j:j + 1, lanes], (CONV_ROWS, LANES)) for j in range(CONV_WIDTH)]
        bias = jnp.broadcast_to(cb_ref[:, lanes], (CONV_ROWS, LANES))
        for k in range(tm // CONV_ROWS):
            r0 = k * CONV_ROWS
            accs = [bias] + [None] * (CONV_CHAINS - 1)
            for j in range(CONV_WIDTH):
                s, blk = (off + j) % SUB, (off + j) // SUB
                term = taps[j] * sh_ref[s, pl.ds(r0 + SUB * blk, CONV_ROWS), lanes]
                a = accs[j % CONV_CHAINS]
                accs[j % CONV_CHAINS] = term if a is None else a + term
            while len(accs) > 1:
                accs = [accs[u] + accs[u + 1] for u in range(0, len(accs), 2)]
            conv_ref[pl.ds(r0, CONV_ROWS), lanes] = accs[0]

    def norm_chunk(k, carry):
        r0 = pl.multiple_of(k * NORM_ROWS, NORM_ROWS)
        acc = conv_ref[pl.ds(r0, NORM_ROWS), :]
        mu = jnp.mean(acc, axis=-1, keepdims=True)
        xc = acc - mu
        y = xc * lax.rsqrt(jnp.mean(xc * xc, axis=-1, keepdims=True) + EPS)
        y = y * lg_ref[...] + lb_ref[...]
        act_ref[pl.ds(r0, NORM_ROWS), :] = (y * jax.nn.sigmoid(y)).astype(act_ref.dtype)
        return carry

    lax.fori_loop(0, tm // NORM_ROWS, norm_chunk, 0)

    for g, (o_g, l_g) in enumerate(((o0_ref, l0_ref), (o1_ref, l1_ref), (o2_ref, l2_ref))):
        dil = DILATED_PATTERNS[g][1]
        lt = tm // dil
        for r in range(dil):
            rows = pl.ds(r, lt, stride=dil) if dil > 1 else pl.ds(0, tm)
            o_r = o_g[r].astype(F32)
            for c in range(ATT_WIDTH // LANES):
                on_ref[g, c, rows, :] = o_r[:, c * LANES:(c + 1) * LANES]
            ln_ref[g, rows, :] = l_g[r]
    lses = [ln_ref[g] for g in range(N_PATTERNS)]
    mx = functools.reduce(jnp.maximum, lses)
    es = [jnp.exp(l - mx) for l in lses]
    tot = functools.reduce(lambda a, b: a + b, es)
    erow = lax.broadcasted_iota(jnp.int32, (LANES, ATT_WIDTH), 0)
    ecol = lax.broadcasted_iota(jnp.int32, (LANES, ATT_WIDTH), 1)
    expand = (ecol // ATT_HEAD_DIM == erow).astype(BF16)
    att = None
    for g in range(N_PATTERNS):
        wgt = es[g] / tot
        hi = wgt.astype(BF16)
        lo = (wgt - hi.astype(F32)).astype(BF16)
        wide = jnp.dot(hi, expand, preferred_element_type=F32) + jnp.dot(lo, expand, preferred_element_type=F32)
        term = wide * jnp.concatenate([on_ref[g, c] for c in range(ATT_WIDTH // LANES)], axis=1)
        att = term if att is None else att + term

    mix = jnp.dot(act_ref[...], w_ref[0:CONV_CH, :], preferred_element_type=F32)
    mix = mix + jnp.dot(att.astype(BF16), w_ref[CONV_CH:, :], preferred_element_type=F32)
    o_ref[...] = x_ref[...] + mod_ref[2:3, :] * mix


def _ab_out_proj(a, att_parts, x, mod, n_off, conv_w, conv_b, ln_g, ln_b, w_out):
    N, S, D = x.shape
    tm = min(ROW_TILE, S)
    nt = S // tm
    hb = tm // CONV_HALO
    n_halo = S // CONV_HALO
    row = lambda n, i: (n, i, 0)
    const = lambda n, i: (0, 0)
    att_specs, att_args = [], []
    for (_, dil), (o_g, l_g) in zip(DILATED_PATTERNS, att_parts):
        att_specs.append(pl.BlockSpec((None, dil, tm // dil, ATT_WIDTH), lambda n, i: (n, 0, i, 0)))
        att_specs.append(pl.BlockSpec((None, dil, tm // dil, LANES), lambda n, i: (n, 0, i, 0)))
        att_args += [o_g, l_g]
    return pl.pallas_call(
        functools.partial(_ab_out_kernel, tm=tm, n_tiles=nt),
        grid=(N, nt),
        in_specs=[
            pl.BlockSpec((None, CONV_HALO, CONV_CH), lambda n, i: (n, jnp.maximum(i * hb - 1, 0), 0)),
            pl.BlockSpec((None, tm, CONV_CH), row),
            pl.BlockSpec((None, CONV_HALO, CONV_CH), lambda n, i: (n, jnp.minimum((i + 1) * hb, n_halo - 1), 0)),
        ] + att_specs + [
            pl.BlockSpec((None, tm, D), row),
            pl.BlockSpec((None, None, 6, D), lambda n, i: (0, n + n_off, 0, 0)),
            pl.BlockSpec((CONV_WIDTH, CONV_CH), const),
            pl.BlockSpec((1, CONV_CH), const),
            pl.BlockSpec((1, CONV_CH), const),
            pl.BlockSpec((1, CONV_CH), const),
            pl.BlockSpec((D, D), const),
        ],
        out_specs=pl.BlockSpec((None, tm, D), row),
        out_shape=jax.ShapeDtypeStruct((N, S, D), F32),
        scratch_shapes=[
            pltpu.VMEM((8, tm + 2 * CONV_HALO, CONV_CH), F32),
            pltpu.VMEM((tm, CONV_CH), F32),
            pltpu.VMEM((tm, CONV_CH), BF16),
            pltpu.VMEM((N_PATTERNS, ATT_WIDTH // LANES, tm, LANES), F32),
            pltpu.VMEM((N_PATTERNS, tm, LANES), F32),
        ],
        compiler_params=_params("parallel", "parallel"),
        name="ab_out_proj",
    )(a, a, a, *att_args, x, mod, conv_w, conv_b, ln_g, ln_b, w_out)


def _mlp_kernel(x_ref, mod_ref, gain_ref, w1_ref, w2_ref, *rest, final):
    if final:
        fg_ref, o_ref = rest
    else:
        (o_ref,) = rest
    x = x_ref[...]
    h = _norm_mod(x, gain_ref[...], mod_ref[4:5, :], mod_ref[3:4, :]).astype(BF16)
    acc = jnp.zeros(x.shape, F32)
    for c in range(D_FF // MLP_FF_TILE):
        sl = slice(c * MLP_FF_TILE, (c + 1) * MLP_FF_TILE)
        u = jnp.maximum(jnp.dot(h, w1_ref[:, sl], preferred_element_type=F32), 0.0)
        acc = acc + jnp.dot((u * u).astype(BF16), w2_ref[sl, :], preferred_element_type=F32)
    y = x + mod_ref[5:6, :] * acc
    if final:
        y = y * lax.rsqrt(jnp.mean(y * y, axis=-1, keepdims=True) + EPS) * fg_ref[...]
    o_ref[...] = y


def _mlp(x, mod, n_off, layer, gain, w1, w2, final_gain=None):
    N, S, D = x.shape
    tm = min(ROW_TILE, S)
    row = lambda n, i: (n, i, 0)
    const = lambda n, i: (0, 0)
    final = final_gain is not None
    in_specs = [
        pl.BlockSpec((None, tm, D), row),
        pl.BlockSpec((None, None, 6, D), lambda n, i: (layer, n + n_off, 0, 0)),
        pl.BlockSpec((1, D), const),
        pl.BlockSpec((D, D_FF), const),
        pl.BlockSpec((D_FF, D), const),
    ]
    args = [x, mod, gain, w1, w2]
    if final:
        in_specs.append(pl.BlockSpec((1, D), const))
        args.append(final_gain)
    return pl.pallas_call(
        functools.partial(_mlp_kernel, final=final),
        grid=(N, S // tm),
        in_specs=in_specs,
        out_specs=pl.BlockSpec((None, tm, D), row),
        out_shape=jax.ShapeDtypeStruct((N, S, D), F32),
        compiler_params=_params("parallel", "parallel"),
        name=f"mlp_{layer}",
    )(*args)


def _c_in_kernel(x_ref, mod_ref, gain_ref, w_ref, wkt_ref, gb_ref, q_ref, kt_ref, v_ref, og_ref, gate_ref):
    W = MLSTM_WIDTH
    x = x_ref[...]
    h = _norm_mod(x, gain_ref[...], mod_ref[1:2, :], mod_ref[0:1, :]).astype(BF16)

    def seg(lo, width):
        return jnp.dot(h, w_ref[:, lo:lo + width], preferred_element_type=F32)

    q_ref[...] = seg(0, W).astype(q_ref.dtype)
    kt = lax.dot_general(wkt_ref[...], h, (((1,), (1,)), ((), ())), preferred_element_type=F32)
    kt_ref[...] = (kt * (MLSTM_HEAD_DIM ** -0.5)).astype(kt_ref.dtype)
    v_ref[...] = seg(2 * W, W).astype(v_ref.dtype)
    og_ref[...] = jax.nn.sigmoid(seg(3 * W, W)).astype(og_ref.dtype)
    z = seg(4 * W, GATE_PAD) + gb_ref[...]
    lane = lax.broadcasted_iota(jnp.int32, (1, GATE_PAD), 1)
    is_forget = (lane % (2 * MLSTM_HEADS)) >= MLSTM_HEADS
    log_sig = jnp.minimum(z, 0.0) - jnp.log1p(jnp.exp(-jnp.abs(z)))
    gate_ref[...] = jnp.where(is_forget, log_sig, z)


def _c_in_proj(x, mod, n_off, gain, w_in, wk_t, gate_b):
    N, S, D = x.shape
    W = MLSTM_WIDTH
    tm = min(ROW_TILE, S)
    row = lambda n, i: (n, i, 0)
    const = lambda n, i: (0, 0)
    return pl.pallas_call(
        _c_in_kernel,
        grid=(N, S // tm),
        in_specs=[
            pl.BlockSpec((None, tm, D), row),
            pl.BlockSpec((None, None, 6, D), lambda n, i: (1, n + n_off, 0, 0)),
            pl.BlockSpec((1, D), const),
            pl.BlockSpec((D, 4 * W + GATE_PAD), const),
            pl.BlockSpec((W, D), const),
            pl.BlockSpec((1, GATE_PAD), const),
        ],
        out_specs=[
            pl.BlockSpec((None, tm, W), row),
            pl.BlockSpec((None, W, tm), lambda n, i: (n, 0, i)),
            pl.BlockSpec((None, tm, W), row),
            pl.BlockSpec((None, tm, W), row),
            pl.BlockSpec((None, tm, GATE_PAD), row),
        ],
        out_shape=[
            jax.ShapeDtypeStruct((N, S, W), BF16),
            jax.ShapeDtypeStruct((N, W, S), BF16),
            jax.ShapeDtypeStruct((N, S, W), BF16),
            jax.ShapeDtypeStruct((N, S, W), BF16),
            jax.ShapeDtypeStruct((N, S, GATE_PAD), F32),
        ],
        compiler_params=_params("parallel", "parallel"),
        name="c_in_proj",
    )(x, mod, gain, w_in, wk_t, gate_b)


def _split3(x):
    hi = x.astype(BF16)
    r1 = x - hi.astype(F32)
    mid = r1.astype(BF16)
    lo = (r1 - mid.astype(F32)).astype(BF16)
    return hi, mid, lo


def _mlstm_kernel(qf_ref, kf_ref, vf_ref, gf_ref, qb_ref, kb_ref, vb_ref, gb_ref,
                  hf_ref, hb_ref, st_ref, m_ref):
    L = MLSTM_CHUNK
    dh = MLSTM_HEAD_DIM
    H = MLSTM_HEADS

    @pl.when(pl.program_id(1) == 0)
    def _():
        st_ref[...] = jnp.zeros_like(st_ref)
        m_ref[...] = jnp.zeros_like(m_ref)

    row = lax.broadcasted_iota(jnp.int32, (L, L), 0)
    col = lax.broadcasted_iota(jnp.int32, (L, L), 1)
    ones_blk = jnp.ones((L, dh), BF16)

    dirs = (
        (qf_ref, kf_ref, vf_ref, gf_ref, hf_ref, row >= col, L - 1),
        (qb_ref, kb_ref, vb_ref, gb_ref, hb_ref, row <= col, 0),
    )
    pre = []
    for d, (q_ref, kt_ref, v_ref, g_ref, h_ref, allowed, last) in enumerate(dirs):
        gates = g_ref[...]
        tri = allowed.astype(BF16)
        csum = sum(jnp.dot(tri, piece, preferred_element_type=F32) for piece in _split3(gates))
        gates_t = gates.T
        csum_t = csum.T
        lo = 2 * H * d
        ig_t = gates_t[lo:lo + H, :]
        b_t = csum_t[lo + H:lo + 2 * H, :]
        crow = ig_t - b_t
        b_last = b_t[:, last:last + 1]
        m_old = m_ref[d]
        wlog = b_last + crow
        m_new = jnp.maximum(b_last + m_old, jnp.max(wlog, axis=-1, keepdims=True))
        decay = jnp.exp(b_last + m_old - m_new)
        wrow = jnp.exp(wlog - m_new)
        m_ref[d] = m_new
        pre.append((csum, crow, m_old, decay, wrow))

    units = [(d, hd) for hd in range(H) for d in range(2)]

    stage_a = []
    for d, hd in units:
        q_ref, kt_ref, v_ref, g_ref, h_ref, allowed, last = dirs[d]
        csum, crow, m_old, decay, wrow = pre[d]
        sl = slice(hd * dh, (hd + 1) * dh)
        cf = 2 * H * d + H + hd
        q = q_ref[:, sl]
        kt = kt_ref[sl, :]
        bb = jnp.broadcast_to(csum[:, cf:cf + 1], (L, L))
        log_d = jnp.where(allowed, bb + crow[hd:hd + 1, :], NEG_INF)
        inter = bb + m_old[hd:hd + 1, :]
        mt = jnp.maximum(inter, jnp.max(log_d, axis=-1, keepdims=True))
        e_intra = jnp.exp(log_d - mt)
        e_inter = jnp.exp(inter - mt)
        sc = jnp.dot(q, kt, preferred_element_type=F32) * e_intra
        lhs = jnp.concatenate([sc.astype(BF16), (e_inter * q.astype(F32)).astype(BF16)], axis=1)
        stage_a.append((lhs, jnp.exp(-mt)))

    for (d, hd), (lhs, emt) in zip(units, stage_a):
        q_ref, kt_ref, v_ref, g_ref, h_ref, allowed, last = dirs[d]
        sl = slice(hd * dh, (hd + 1) * dh)
        v1 = jnp.concatenate([v_ref[:, sl], ones_blk], axis=1)
        rhs = jnp.concatenate([v1, st_ref[d, hd].astype(BF16)], axis=0)
        nd = jnp.dot(lhs, rhs, preferred_element_type=F32)
        den = jnp.maximum(jnp.abs(nd[:, dh:]), emt)
        h_ref[:, sl] = (nd[:, 0:dh] / den).astype(h_ref.dtype)

    for d, hd in units:
        q_ref, kt_ref, v_ref, g_ref, h_ref, allowed, last = dirs[d]
        csum, crow, m_old, decay, wrow = pre[d]
        sl = slice(hd * dh, (hd + 1) * dh)
        v1 = jnp.concatenate([v_ref[:, sl], ones_blk], axis=1)
        kw = (kt_ref[sl, :].astype(F32) * wrow[hd:hd + 1, :]).astype(BF16)
        dec = decay[hd:hd + 1, :]
        upd = jnp.dot(kw, v1, preferred_element_type=F32)
        st_ref[d, hd] = jnp.concatenate([dec, dec], axis=1) * st_ref[d, hd] + upd


def _mlstm_scan(q, kt, v, gates):
    N, S, W = q.shape
    L = MLSTM_CHUNK
    nc = S // L
    rows_f = lambda n, i: (n, i, 0)
    rows_b = lambda n, i: (n, nc - 1 - i, 0)
    cols_f = lambda n, i: (n, 0, i)
    cols_b = lambda n, i: (n, 0, nc - 1 - i)
    blk = lambda im: pl.BlockSpec((None, L, W), im)
    tblk = lambda im: pl.BlockSpec((None, W, L), im)
    gblk = lambda im: pl.BlockSpec((None, L, GATE_PAD), im)
    return pl.pallas_call(
        _mlstm_kernel,
        grid=(N, nc),
        in_specs=[blk(rows_f), tblk(cols_f), blk(rows_f), gblk(rows_f),
                  blk(rows_b), tblk(cols_b), blk(rows_b), gblk(rows_b)],
        out_specs=[blk(rows_f), blk(rows_b)],
        out_shape=[jax.ShapeDtypeStruct((N, S, W), BF16)] * 2,
        scratch_shapes=[
            pltpu.VMEM((2, MLSTM_HEADS, MLSTM_HEAD_DIM, 2 * MLSTM_HEAD_DIM), F32),
            pltpu.VMEM((2, MLSTM_HEADS, LANES), F32),
        ],
        compiler_params=_params("parallel", "arbitrary"),
        name="mlstm_scan",
    )(q, kt, v, gates, q, kt, v, gates)


def _c_out_kernel(hf_ref, hb_ref, og_ref, x_ref, mod_ref, hn_ref, w_ref, o_ref):
    dh = MLSTM_HEAD_DIM
    parts = []
    for hd in range(MLSTM_HEADS):
        sl = slice(hd * dh, (hd + 1) * dh)
        ht = hf_ref[:, sl].astype(F32) + hb_ref[:, sl].astype(F32)
        ht = ht * lax.rsqrt(jnp.mean(ht * ht, axis=-1, keepdims=True) + EPS)
        parts.append((og_ref[:, sl].astype(F32) * (ht * hn_ref[:, sl])).astype(BF16))
    z = jnp.concatenate(parts, axis=1)
    mix = jnp.dot(z, w_ref[...], preferred_element_type=F32)
    o_ref[...] = x_ref[...] + mod_ref[2:3, :] * mix


def _c_out_proj(hf, hb, og, x, mod, n_off, head_norm, w_out):
    N, S, D = x.shape
    W = MLSTM_WIDTH
    tm = min(ROW_TILE, S)
    row = lambda n, i: (n, i, 0)
    const = lambda n, i: (0, 0)
    return pl.pallas_call(
        _c_out_kernel,
        grid=(N, S // tm),
        in_specs=[
            pl.BlockSpec((None, tm, W), row),
            pl.BlockSpec((None, tm, W), row),
            pl.BlockSpec((None, tm, W), row),
            pl.BlockSpec((None, tm, D), row),
            pl.BlockSpec((None, None, 6, D), lambda n, i: (1, n + n_off, 0, 0)),
            pl.BlockSpec((1, W), const),
            pl.BlockSpec((W, D), const),
        ],
        out_specs=pl.BlockSpec((None, tm, D), row),
        out_shape=jax.ShapeDtypeStruct((N, S, D), F32),
        compiler_params=_params("parallel", "parallel"),
        name="c_out_proj",
    )(hf, hb, og, x, mod, head_norm, w_out)


def _trunk(x, mod, n_off, p):
    N, S, D = x.shape
    tables = _rope_tables(S)
    a, *qkv = _ab_in_proj(x, mod, n_off, p["norm_mix0"], p["ab_w_in"], tables)
    att_parts = [_dilated_attention(qkv[g], g) for g in range(N_PATTERNS)]
    x = _ab_out_proj(a, att_parts, x, mod, n_off, p["conv_w"], p["conv_b"], p["conv_ln_g"],
                     p["conv_ln_b"], p["ab_w_out"])
    x = _mlp(x, mod, n_off, 0, p["norm_mlp0"], p["mlp_w1_0"], p["mlp_w2_0"])
    q1, kt1, v1, og, gates = _c_in_proj(x, mod, n_off, p["norm_mix1"], p["c_w_in"], p["c_wk_t"], p["c_gate_b"])
    hf, hb = _mlstm_scan(q1, kt1, v1, gates)
    x = _c_out_proj(hf, hb, og, x, mod, n_off, p["c_head_norm"], p["c_w_out"])
    return _mlp(x, mod, n_off, 1, p["norm_mlp1"], p["mlp_w1_1"], p["mlp_w2_1"], p["norm_final"])


def kernel(x_prompt, x_sample, c_prompt, c_sample, ada_w, ada_b, norm_mix, norm_mlp, ab_w_in, conv_w, conv_b, conv_ln_g, conv_ln_b, ab_w_out, c_w_in, c_gate_b, c_head_norm, c_w_out, mlp_w1, mlp_w2, norm_final):
    D = D_MODEL
    n_p, n_s = c_prompt.shape[0], c_sample.shape[0]
    rows = n_p + n_s
    rows_pad = -(-rows // 8) * 8
    c_all = jnp.concatenate([c_prompt, c_sample, jnp.zeros((rows_pad - rows, D), F32)], axis=0)
    mod = _modulation(c_all, ada_w, ada_b)

    W = MLSTM_WIDTH
    pad = GATE_PAD - N_GATES
    p = {
        "norm_mix0": norm_mix[0].reshape(1, D),
        "norm_mix1": norm_mix[1].reshape(1, D),
        "norm_mlp0": norm_mlp[0].reshape(1, D),
        "norm_mlp1": norm_mlp[1].reshape(1, D),
        "ab_w_in": ab_w_in[0].astype(BF16),
        "conv_w": conv_w[0],
        "conv_b": conv_b[0].reshape(1, CONV_CH),
        "conv_ln_g": conv_ln_g[0].reshape(1, CONV_CH),
        "conv_ln_b": conv_ln_b[0].reshape(1, CONV_CH),
        "ab_w_out": ab_w_out[0].astype(BF16),
        "c_w_in": jnp.pad(c_w_in[0], ((0, 0), (0, pad))).astype(BF16),
        "c_wk_t": c_w_in[0][:, W:2 * W].T.astype(BF16),
        "c_gate_b": jnp.pad(c_gate_b[0], (0, pad)).reshape(1, GATE_PAD),
        "c_head_norm": c_head_norm[0].reshape(1, W),
        "c_w_out": c_w_out[0].astype(BF16),
        "mlp_w1_0": mlp_w1[0].astype(BF16),
        "mlp_w2_0": mlp_w2[0].astype(BF16),
        "mlp_w1_1": mlp_w1[1].astype(BF16),
        "mlp_w2_1": mlp_w2[1].astype(BF16),
        "norm_final": norm_final.reshape(1, D),
    }
    y_prompt = _trunk(x_prompt, mod, 0, p)
    y_sample = _trunk(x_sample, mod, n_p, p)
    return (y_prompt, y_sample)
```

```python
import functools

import jax
import jax.numpy as jnp
from jax import lax
from jax.experimental import pallas as pl
from jax.experimental.pallas import tpu as pltpu

D_MODEL = 1024
DEPTH = 2
CONV_CH = 512
CONV_WIDTH = 31
CONV_HALO = 16
ATT_HEADS = 8
ATT_HEAD_DIM = 64
ATT_WIDTH = ATT_HEADS * ATT_HEAD_DIM
DILATED_PATTERNS = ((128, 1), (512, 4), (2048, 16))
N_PATTERNS = len(DILATED_PATTERNS)
ATT_RADIUS = 64
ROPE_THETA = 500000.0
ROPE_DIM = ATT_HEAD_DIM // 4
ROPE_HALF = ROPE_DIM // 2
AB_IN = 2 * CONV_CH + N_PATTERNS * 3 * ATT_WIDTH
QKV_W = N_PATTERNS * 3 * ATT_WIDTH
MLSTM_HEADS = 8
MLSTM_WIDTH = D_MODEL
MLSTM_HEAD_DIM = MLSTM_WIDTH // MLSTM_HEADS
MLSTM_CHUNK = 128
N_GATES = 4 * MLSTM_HEADS
GATE_PAD = 128
D_FF = 4 * D_MODEL
EPS = 1e-6
NEG_INF = -1e30
LOG2_E = 1.4426950408889634
LN_2 = 0.6931471805599453

LANES = 128
VMEM_LIMIT = 56 * 1024 * 1024
ROW_TILE = 512
ATT_TQ = 128
ATT_TILES_PER_STEP = 4
CONV_ROWS = 32
CONV_CHAINS = 4
NORM_ROWS = 128
MLP_FF_TILE = 1024

F32 = jnp.float32
BF16 = jnp.bfloat16


def _params(*sem):
    return pltpu.CompilerParams(dimension_semantics=sem, vmem_limit_bytes=VMEM_LIMIT)


def _norm_mod(x, gain, scale, shift):
    y = x * lax.rsqrt(jnp.mean(x * x, axis=-1, keepdims=True) + EPS)
    return (y * gain) * (1.0 + scale) + shift


def _mod_kernel(c_ref, w_ref, b_ref, o_ref):
    c = c_ref[...]
    s = c * jax.nn.sigmoid(c)
    o_ref[...] = jnp.dot(s, w_ref[...], preferred_element_type=F32,
                         precision=lax.Precision.HIGHEST) + b_ref[...]


def _modulation(c_all, ada_w, ada_b):
    R = c_all.shape[0]
    D = D_MODEL
    out = pl.pallas_call(
        _mod_kernel,
        grid=(DEPTH, 6),
        in_specs=[
            pl.BlockSpec((R, D), lambda l, j: (0, 0)),
            pl.BlockSpec((None, D, D), lambda l, j: (l, 0, j)),
            pl.BlockSpec((None, 1, D), lambda l, j: (l, 0, j)),
        ],
        out_specs=pl.BlockSpec((None, R, D), lambda l, j: (l, 0, j)),
        out_shape=jax.ShapeDtypeStruct((DEPTH, R, 6 * D), F32),
        compiler_params=_params("arbitrary", "arbitrary"),
        name="adaln_mod",
    )(c_all, ada_w, ada_b.reshape(DEPTH, 1, 6 * D))
    return out.reshape(DEPTH, R, 6, D)


def _rope_tables(S):
    inv_freq = jnp.power(ROPE_THETA, -jnp.arange(ROPE_HALF, dtype=F32) / ROPE_HALF)
    ang = jnp.arange(S).astype(F32)[:, None] * inv_freq[None, :]
    cos, sin = jnp.cos(ang), jnp.sin(ang)
    ones = jnp.ones((S, ATT_HEAD_DIM - ROPE_DIM), F32)
    zeros = jnp.zeros((S, ATT_HEAD_DIM - ROPE_DIM), F32)
    zh = jnp.zeros((S, ROPE_HALF), F32)
    cos_t = jnp.concatenate([cos, cos, ones], axis=1)
    sin_up = jnp.concatenate([-sin, zh, zeros], axis=1)
    sin_dn = jnp.concatenate([zh, sin, zeros], axis=1)
    rep = LANES // ATT_HEAD_DIM
    return tuple(jnp.tile(t, (1, rep)) for t in (cos_t, sin_up, sin_dn))


def _ab_in_kernel(x_ref, mod_ref, gain_ref, w_ref, *rest, tm):
    tab_refs = rest[:3]
    a_ref = rest[3]
    out_refs = rest[4:4 + N_PATTERNS]
    h_ref = rest[-1]
    hf = _norm_mod(x_ref[...], gain_ref[...], mod_ref[1:2, :], mod_ref[0:1, :])
    n_slab = hf.shape[1] // LANES
    for c in range(n_slab):
        h_ref[c] = hf[:, c * LANES:(c + 1) * LANES]
    h = hf.astype(BF16)

    def seg(lhs, j):
        return jnp.dot(lhs, w_ref[:, j * ATT_WIDTH:(j + 1) * ATT_WIDTH], preferred_element_type=F32)

    a_ref[...] = (seg(h, 0) * jax.nn.sigmoid(seg(h, 1))).astype(a_ref.dtype)

    def rope(p, tabs, scale):
        cos_t, sin_up, sin_dn = tabs
        parts = []
        for j in range(ATT_WIDTH // LANES):
            t = p[:, j * LANES:(j + 1) * LANES]
            up = pltpu.roll(t, LANES - ROPE_HALF, 1)
            dn = pltpu.roll(t, ROPE_HALF, 1)
            r = t * cos_t + up * sin_up + dn * sin_dn
            if scale != 1.0:
                r = r * scale
            parts.append(r.astype(BF16))
        return jnp.concatenate(parts, axis=1)

    for g, (_, dil) in enumerate(DILATED_PATTERNS):
        lt = tm // dil
        if dil == 1:
            hg = h
            tabs = tuple(t[...] for t in tab_refs)
        else:
            hg = jnp.concatenate(
                [jnp.concatenate([h_ref[c, pl.ds(r, lt, stride=dil), :] for c in range(n_slab)], axis=1)
                 for r in range(dil)], axis=0).astype(BF16)
            tabs = tuple(jnp.concatenate([t[pl.ds(r, lt, stride=dil), :] for r in range(dil)], axis=0)
                         for t in tab_refs)
        q = rope(seg(hg, 2 + 3 * g), tabs, ATT_HEAD_DIM ** -0.5 * LOG2_E)
        k = rope(seg(hg, 3 + 3 * g), tabs, 1.0)
        v = seg(hg, 4 + 3 * g).astype(BF16)
        for c, val in enumerate((q, k, v)):
            for r in range(dil):
                out_refs[g][r, :, c * ATT_WIDTH:(c + 1) * ATT_WIDTH] = val[r * lt:(r + 1) * lt, :]


def _ab_in_proj(x, mod, n_off, gain, w_in, tables):
    N, S, D = x.shape
    tm = min(ROW_TILE, S)
    tab_spec = pl.BlockSpec((tm, LANES), lambda n, i: (i, 0))
    out_specs = [pl.BlockSpec((None, tm, CONV_CH), lambda n, i: (n, i, 0))]
    out_shape = [jax.ShapeDtypeStruct((N, S, CONV_CH), BF16)]
    for _, dil in DILATED_PATTERNS:
        out_specs.append(pl.BlockSpec((None, dil, tm // dil, 3 * ATT_WIDTH), lambda n, i: (n, 0, i, 0)))
        out_shape.append(jax.ShapeDtypeStruct((N, dil, S // dil, 3 * ATT_WIDTH), BF16))
    return pl.pallas_call(
        functools.partial(_ab_in_kernel, tm=tm),
        grid=(N, S // tm),
        in_specs=[
            pl.BlockSpec((None, tm, D), lambda n, i: (n, i, 0)),
            pl.BlockSpec((None, None, 6, D), lambda n, i: (0, n + n_off, 0, 0)),
            pl.BlockSpec((1, D), lambda n, i: (0, 0)),
            pl.BlockSpec((D, AB_IN), lambda n, i: (0, 0)),
        ] + [tab_spec] * 3,
        out_specs=out_specs,
        out_shape=out_shape,
        scratch_shapes=[pltpu.VMEM((D // LANES, tm, LANES), F32)],
        compiler_params=_params("parallel", "parallel"),
        name="ab_in_proj",
    )(x, mod, gain, w_in, *tables)


def _attn_kernel(q_ref, kp_ref, kc_ref, kn_ref, vp_ref, vc_ref, vn_ref, o_ref, l_ref, *, tq, n_sub, n_seq, n_blocks):
    i = pl.program_id(2)
    R = ATT_RADIUS
    bq = n_sub * tq
    tk = tq + 2 * R
    row = lax.broadcasted_iota(jnp.int32, (tq, tk), 0)
    col = lax.broadcasted_iota(jnp.int32, (tq, tk), 1)
    rel = col - row
    band = (rel >= 0) & (rel <= 2 * R)
    masks = []
    for t in range(n_sub):
        mask = band
        if t == 0:
            mask = mask & (col >= jnp.where(i == 0, R, 0))
        if t == n_sub - 1:
            mask = mask & (col < jnp.where(i == n_blocks - 1, tq + R, tk))
        masks.append(mask)
    lane = lax.broadcasted_iota(jnp.int32, (1, LANES), 1)
    first_head = lane < ATT_HEAD_DIM

    def window(p_ref, c_ref, n_ref, r, t, sl):
        lo, hi = t * tq - R, (t + 1) * tq + R
        parts = []
        if lo < 0:
            parts.append(p_ref[r, :, sl])
        parts.append(c_ref[r, max(lo, 0):min(hi, bq), sl])
        if hi > bq:
            parts.append(n_ref[r, :, sl])
        return jnp.concatenate(parts, axis=0)

    for r in range(n_seq):
        for t in range(n_sub):
            rows = slice(t * tq, (t + 1) * tq)
            lse_all = jnp.zeros((tq, LANES), F32)
            for j in range(ATT_WIDTH // LANES):
                sl = slice(j * LANES, (j + 1) * LANES)
                q = q_ref[r, rows, sl]
                kw = window(kp_ref, kc_ref, kn_ref, r, t, sl)
                vw = window(vp_ref, vc_ref, vn_ref, r, t, sl)
                o_pair = None
                for hh in range(2):
                    sel = first_head if hh == 0 else jnp.logical_not(first_head)
                    qm = jnp.where(sel, q, jnp.zeros_like(q))
                    s = lax.dot_general(qm, kw, (((1,), (1,)), ((), ())), preferred_element_type=F32)
                    s = jnp.where(masks[t], s, NEG_INF)
                    m = jnp.max(s, axis=-1, keepdims=True)
                    p = jnp.exp2(s - m)
                    l = jnp.sum(p, axis=-1, keepdims=True)
                    o_h = jnp.dot(p.astype(vw.dtype), vw, preferred_element_type=F32) / l
                    o_pair = o_h if hh == 0 else jnp.where(first_head, o_pair, o_h)
                    lse_all = jnp.where(lane == 2 * j + hh, (m + jnp.log2(l)) * LN_2, lse_all)
                o_ref[r, rows, sl] = o_pair.astype(o_ref.dtype)
            l_ref[r, rows, :] = lse_all


def _dilated_attention(qkv_g, g):
    N, dil, L, _ = qkv_g.shape
    tq = min(ATT_TQ, L)
    n_sub = min(ATT_TILES_PER_STEP, L // tq)
    n_seq = min(ATT_TILES_PER_STEP // n_sub, dil)
    bq = n_sub * tq
    nb = L // bq
    hb = bq // ATT_RADIUS
    n_halo = L // ATT_RADIUS

    def cur(c):
        return pl.BlockSpec((None, n_seq, bq, ATT_WIDTH), lambda n, r, i: (n, r, i, c))

    def before(c):
        return pl.BlockSpec((None, n_seq, ATT_RADIUS, ATT_WIDTH),
                            lambda n, r, i: (n, r, jnp.maximum(i * hb - 1, 0), c))

    def after(c):
        return pl.BlockSpec((None, n_seq, ATT_RADIUS, ATT_WIDTH),
                            lambda n, r, i: (n, r, jnp.minimum((i + 1) * hb, n_halo - 1), c))

    return pl.pallas_call(
        functools.partial(_attn_kernel, tq=tq, n_sub=n_sub, n_seq=n_seq, n_blocks=nb),
        grid=(N, dil // n_seq, nb),
        in_specs=[cur(0), before(1), cur(1), after(1), before(2), cur(2), after(2)],
        out_specs=[
            pl.BlockSpec((None, n_seq, bq, ATT_WIDTH), lambda n, r, i: (n, r, i, 0)),
            pl.BlockSpec((None, n_seq, bq, LANES), lambda n, r, i: (n, r, i, 0)),
        ],
        out_shape=[
            jax.ShapeDtypeStruct((N, dil, L, ATT_WIDTH), BF16),
            jax.ShapeDtypeStruct((N, dil, L, LANES), F32),
        ],
        compiler_params=_params("parallel", "parallel", "parallel"),
        name=f"dilated_attn_{g}",
    )(*([qkv_g] * 7))


def _interleave(main, side):
    done = 0
    for k, thunk in enumerate(main):
        thunk()
        upto = (len(side) * (k + 1)) // len(main)
        for s in side[done:upto]:
            s()
        done = upto


def _ab_tail_kernel(ap_ref, ac_ref, an_ref, o0_ref, l0_ref, o1_ref, l1_ref, o2_ref, l2_ref, x_ref, mod_ref,
                    cw_ref, cb_ref, lg_ref, lb_ref, wo_ref, gain_ref, w1_ref, w2_ref, o_ref,
                    sh_ref, conv_ref, on_ref, ln_ref, act_ref, att_ref, x1_ref, h2_ref, u_ref, acc_ref,
                    *, tm, nt, n_tiles):
    t = pl.program_id(0)
    H = CONV_HALO
    SUB = 8
    i = jnp.minimum(t, n_tiles - 1) % nt
    w_slot = t % 2
    r_slot = 1 - w_slot

    @pl.when(t == 0)
    def _():
        act_ref[...] = jnp.zeros_like(act_ref)
        att_ref[...] = jnp.zeros_like(att_ref)

    tokens = []

    def token_of(value):
        bits = pltpu.bitcast(value.astype(F32), jnp.int32)
        rows = [bits[r:r + SUB, c:c + LANES] for r in range(0, bits.shape[0], SUB)
                for c in range(0, bits.shape[1], LANES)]
        tokens.append(functools.reduce(jnp.bitwise_or, rows))

    def relu_floor(shape):
        if not tokens:
            return 0.0
        tok = functools.reduce(jnp.bitwise_or, tokens)
        tokens.clear()
        zero = lax.shift_right_logical(lax.shift_right_logical(tok, 16), 16).astype(F32)
        return jnp.broadcast_to(zero[0:1, 0:1], shape)

    NB = 256
    RB = tm // 2

    def out_proj():
        mix = jnp.dot(act_ref[r_slot], wo_ref[0:CONV_CH, :], preferred_element_type=F32)
        mix = mix + jnp.dot(att_ref[r_slot], wo_ref[CONV_CH:, :], preferred_element_type=F32)
        x1 = x_ref[...] + mod_ref[2:3, :] * mix
        x1_ref[...] = x1
        h2_ref[...] = _norm_mod(x1, gain_ref[...], mod_ref[4:5, :], mod_ref[3:4, :]).astype(BF16)

    def up(c, j, h):
        def run():
            lo = c * MLP_FF_TILE + j * NB
            rows = slice(h * RB, (h + 1) * RB)
            u = jnp.dot(h2_ref[rows, :], w1_ref[:, lo:lo + NB], preferred_element_type=F32)
            u = jnp.maximum(u, relu_floor(u.shape))
            u_ref[rows, j * NB:(j + 1) * NB] = (u * u).astype(BF16)
        return run

    def down(c, j, h):
        def run():
            krows = slice(c * MLP_FF_TILE, (c + 1) * MLP_FF_TILE)
            cols = slice(j * NB, (j + 1) * NB)
            rows = slice(h * RB, (h + 1) * RB)
            part = jnp.dot(u_ref[rows, :], w2_ref[krows, cols], preferred_element_type=F32)
            if c == 0:
                acc_ref[rows, cols] = part
            else:
                acc_ref[rows, cols] += part
        return run

    main = [out_proj]
    for c in range(D_FF // MLP_FF_TILE):
        main += [up(c, j, h) for j in range(MLP_FF_TILE // NB) for h in range(2)]
        main += [down(c, j, h) for j in range(D_MODEL // NB) for h in range(2)]

    off = H - CONV_WIDTH // 2
    span = tm + 2 * H - SUB

    def fill():
        sh_ref[0, 0:H, :] = jnp.where(i > 0, ap_ref[...].astype(F32), 0.0)
        sh_ref[0, H:H + tm, :] = ac_ref[...].astype(F32)
        sh_ref[0, H + tm:H + tm + H, :] = jnp.where(i < nt - 1, an_ref[...].astype(F32), 0.0)

    def shift(s):
        def run():
            sh_ref[s, 0:span, :] = sh_ref[0, s:s + span, :]
        return run

    def conv(c, k):
        def run():
            lanes = slice(c * LANES, (c + 1) * LANES)
            r0 = k * CONV_ROWS
            accs = [jnp.broadcast_to(cb_ref[:, lanes], (CONV_ROWS, LANES))] + [None] * (CONV_CHAINS - 1)
            for j in range(CONV_WIDTH):
                s, blk = (off + j) % SUB, (off + j) // SUB
                tap = jnp.broadcast_to(cw_ref[j:j + 1, lanes], (CONV_ROWS, LANES))
                term = tap * sh_ref[s, pl.ds(r0 + SUB * blk, CONV_ROWS), lanes]
                a = accs[j % CONV_CHAINS]
                accs[j % CONV_CHAINS] = term if a is None else a + term
            while len(accs) > 1:
                accs = [accs[u] + accs[u + 1] for u in range(0, len(accs), 2)]
            conv_ref[pl.ds(r0, CONV_ROWS), lanes] = accs[0]
            token_of(accs[0])
        return run

    def norm(k):
        def run():
            r0 = k * NORM_ROWS
            acc = conv_ref[pl.ds(r0, NORM_ROWS), :]
            mu = jnp.mean(acc, axis=-1, keepdims=True)
            xc = acc - mu
            y = xc * lax.rsqrt(jnp.mean(xc * xc, axis=-1, keepdims=True) + EPS)
            y = y * lg_ref[...] + lb_ref[...]
            y = y * jax.nn.sigmoid(y)
            act_ref[w_slot, pl.ds(r0, NORM_ROWS), :] = y.astype(act_ref.dtype)
            token_of(y)
        return run

    def unpermute(g, o_g, l_g):
        def run():
            dil = DILATED_PATTERNS[g][1]
            lt = tm // dil
            for r in range(dil):
                rows = pl.ds(r, lt, stride=dil) if dil > 1 else pl.ds(0, tm)
                o_r = o_g[r].astype(F32)
                for c in range(ATT_WIDTH // LANES):
                    on_ref[g, c, rows, :] = o_r[:, c * LANES:(c + 1) * LANES]
                ln_ref[g, rows, :] = l_g[r]
        return run

    def weights():
        lses = [ln_ref[g] for g in range(N_PATTERNS)]
        mx = functools.reduce(jnp.maximum, lses)
        es = [jnp.exp(l - mx) for l in lses]
        tot = functools.reduce(lambda a, b: a + b, es)
        for g in range(N_PATTERNS):
            ln_ref[g] = es[g] / tot

    def combine(c):
        def run():
            erow = lax.broadcasted_iota(jnp.int32, (LANES, LANES), 0)
            ecol = lax.broadcasted_iota(jnp.int32, (LANES, LANES), 1)
            expand = ((ecol + c * LANES) // ATT_HEAD_DIM == erow).astype(BF16)
            merged = None
            for g in range(N_PATTERNS):
                wgt = ln_ref[g]
                hi = wgt.astype(BF16)
                lo = (wgt - hi.astype(F32)).astype(BF16)
                wide = (jnp.dot(hi, expand, preferred_element_type=F32)
                        + jnp.dot(lo, expand, preferred_element_type=F32))
                term = wide * on_ref[g, c]
                merged = term if merged is None else merged + term
            att_ref[w_slot, :, c * LANES:(c + 1) * LANES] = merged.astype(att_ref.dtype)
            token_of(merged)
        return run

    att_refs = ((o0_ref, l0_ref), (o1_ref, l1_ref), (o2_ref, l2_ref))
    side = [fill] + [shift(s) for s in range(1, SUB)]
    side += [conv(c, k) for c in range(CONV_CH // LANES) for k in range(tm // CONV_ROWS)]
    side += [norm(k) for k in range(tm // NORM_ROWS)]
    side += [unpermute(g, o_g, l_g) for g, (o_g, l_g) in enumerate(att_refs)]
    side += [weights] + [combine(c) for c in range(ATT_WIDTH // LANES)]

    _interleave(main, side)
    o_ref[...] = x1_ref[...] + mod_ref[5:6, :] * acc_ref[...] + relu_floor((1, 1))


def _ab_tail(a, att_parts, x, mod, n_off, conv_w, conv_b, ln_g, ln_b, w_out, gain, w1, w2):
    N, S, D = x.shape
    tm = min(ROW_TILE, S)
    nt = S // tm
    n_tiles = N * nt
    hb = tm // CONV_HALO
    n_halo = S // CONV_HALO

    def ta(t):
        tc = jnp.minimum(t, n_tiles - 1)
        return tc // nt, tc % nt

    def tb(t):
        tc = jnp.maximum(t - 1, 0)
        return tc // nt, tc % nt

    const = lambda t: (0, 0)
    once = pl.Buffered(1)
    att_specs, att_args = [], []
    for (_, dil), (o_g, l_g) in zip(DILATED_PATTERNS, att_parts):
        att_specs.append(pl.BlockSpec((None, dil, tm // dil, ATT_WIDTH), lambda t: (ta(t)[0], 0, ta(t)[1], 0)))
        att_specs.append(pl.BlockSpec((None, dil, tm // dil, LANES), lambda t: (ta(t)[0], 0, ta(t)[1], 0)))
        att_args += [o_g, l_g]
    return pl.pallas_call(
        functools.partial(_ab_tail_kernel, tm=tm, nt=nt, n_tiles=n_tiles),
        grid=(n_tiles + 1,),
        in_specs=[
            pl.BlockSpec((None, CONV_HALO, CONV_CH), lambda t: (ta(t)[0], jnp.maximum(ta(t)[1] * hb - 1, 0), 0)),
            pl.BlockSpec((None, tm, CONV_CH), lambda t: (ta(t)[0], ta(t)[1], 0)),
            pl.BlockSpec((None, CONV_HALO, CONV_CH),
                         lambda t: (ta(t)[0], jnp.minimum((ta(t)[1] + 1) * hb, n_halo - 1), 0)),
        ] + att_specs + [
            pl.BlockSpec((None, tm, D), lambda t: (tb(t)[0], tb(t)[1], 0)),
            pl.BlockSpec((None, None, 6, D), lambda t: (0, tb(t)[0] + n_off, 0, 0)),
            pl.BlockSpec((CONV_WIDTH, CONV_CH), const),
            pl.BlockSpec((1, CONV_CH), const),
            pl.BlockSpec((1, CONV_CH), const),
            pl.BlockSpec((1, CONV_CH), const),
            pl.BlockSpec((D, D), const, pipeline_mode=once),
            pl.BlockSpec((1, D), const),
            pl.BlockSpec((D, D_FF), const, pipeline_mode=once),
            pl.BlockSpec((D_FF, D), const, pipeline_mode=once),
        ],
        out_specs=pl.BlockSpec((None, tm, D), lambda t: (tb(t)[0], tb(t)[1], 0)),
        out_shape=jax.ShapeDtypeStruct((N, S, D), F32),
        scratch_shapes=[
            pltpu.VMEM((8, tm + 2 * CONV_HALO, CONV_CH), F32),
            pltpu.VMEM((tm, CONV_CH), F32),
            pltpu.VMEM((N_PATTERNS, ATT_WIDTH // LANES, tm, LANES), F32),
            pltpu.VMEM((N_PATTERNS, tm, LANES), F32),
            pltpu.VMEM((2, tm, CONV_CH), BF16),
            pltpu.VMEM((2, tm, ATT_WIDTH), BF16),
            pltpu.VMEM((tm, D), F32),
            pltpu.VMEM((tm, D), BF16),
            pltpu.VMEM((tm, MLP_FF_TILE), BF16),
            pltpu.VMEM((tm, D), F32),
        ],
        compiler_params=_params("arbitrary"),
        name="ab_tail",
    )(a, a, a, *att_args, x, mod, conv_w, conv_b, ln_g, ln_b, w_out, gain, w1, w2)


def _c_tail_kernel(hf_ref, hb_ref, og_ref, x_ref, mod_ref, hn_ref, wo_ref, gain_ref, w1_ref, w2_ref, fg_ref, o_ref):
    dh = MLSTM_HEAD_DIM
    parts = []
    for hd in range(MLSTM_HEADS):
        sl = slice(hd * dh, (hd + 1) * dh)
        ht = hf_ref[:, sl].astype(F32) + hb_ref[:, sl].astype(F32)
        ht = ht * lax.rsqrt(jnp.mean(ht * ht, axis=-1, keepdims=True) + EPS)
        parts.append((og_ref[:, sl].astype(F32) * (ht * hn_ref[:, sl])).astype(BF16))
    z = jnp.concatenate(parts, axis=1)
    mix = jnp.dot(z, wo_ref[...], preferred_element_type=F32)
    x1 = x_ref[...] + mod_ref[2:3, :] * mix
    h2 = _norm_mod(x1, gain_ref[...], mod_ref[4:5, :], mod_ref[3:4, :]).astype(BF16)
    acc = None
    for c in range(D_FF // MLP_FF_TILE):
        sl = slice(c * MLP_FF_TILE, (c + 1) * MLP_FF_TILE)
        u = jnp.maximum(jnp.dot(h2, w1_ref[:, sl], preferred_element_type=F32), 0.0)
        part = jnp.dot((u * u).astype(BF16), w2_ref[sl, :], preferred_element_type=F32)
        acc = part if acc is None else acc + part
    y = x1 + mod_ref[5:6, :] * acc
    o_ref[...] = y * lax.rsqrt(jnp.mean(y * y, axis=-1, keepdims=True) + EPS) * fg_ref[...]


def _c_tail(hf, hb, og, x, mod, n_off, head_norm, w_out, gain, w1, w2, final_gain):
    N, S, D = x.shape
    W = MLSTM_WIDTH
    tm = min(ROW_TILE, S)
    row = lambda n, i: (n, i, 0)
    const = lambda n, i: (0, 0)
    once = pl.Buffered(1)
    return pl.pallas_call(
        _c_tail_kernel,
        grid=(N, S // tm),
        in_specs=[
            pl.BlockSpec((None, tm, W), row),
            pl.BlockSpec((None, tm, W), row),
            pl.BlockSpec((None, tm, W), row),
            pl.BlockSpec((None, tm, D), row),
            pl.BlockSpec((None, None, 6, D), lambda n, i: (1, n + n_off, 0, 0)),
            pl.BlockSpec((1, W), const),
            pl.BlockSpec((W, D), const, pipeline_mode=once),
            pl.BlockSpec((1, D), const),
            pl.BlockSpec((D, D_FF), const, pipeline_mode=once),
            pl.BlockSpec((D_FF, D), const, pipeline_mode=once),
            pl.BlockSpec((1, D), const),
        ],
        out_specs=pl.BlockSpec((None, tm, D), row),
        out_shape=jax.ShapeDtypeStruct((N, S, D), F32),
        compiler_params=_params("parallel", "parallel"),
        name="c_tail",
    )(hf, hb, og, x, mod, head_norm, w_out, gain, w1, w2, final_gain)


def _c_in_kernel(x_ref, mod_ref, gain_ref, w_ref, wkt_ref, gb_ref, q_ref, kt_ref, v_ref, og_ref, gate_ref):
    W = MLSTM_WIDTH
    x = x_ref[...]
    h = _norm_mod(x, gain_ref[...], mod_ref[1:2, :], mod_ref[0:1, :]).astype(BF16)

    def seg(lo, width):
        return jnp.dot(h, w_ref[:, lo:lo + width], preferred_element_type=F32)

    q_ref[...] = seg(0, W).astype(q_ref.dtype)
    kt = lax.dot_general(wkt_ref[...], h, (((1,), (1,)), ((), ())), preferred_element_type=F32)
    kt_ref[...] = (kt * (MLSTM_HEAD_DIM ** -0.5)).astype(kt_ref.dtype)
    v_ref[...] = seg(2 * W, W).astype(v_ref.dtype)
    og_ref[...] = jax.nn.sigmoid(seg(3 * W, W)).astype(og_ref.dtype)
    z = seg(4 * W, GATE_PAD) + gb_ref[...]
    lane = lax.broadcasted_iota(jnp.int32, (1, GATE_PAD), 1)
    is_forget = (lane % (2 * MLSTM_HEADS)) >= MLSTM_HEADS
    log_sig = jnp.minimum(z, 0.0) - jnp.log1p(jnp.exp(-jnp.abs(z)))
    gate_ref[...] = jnp.where(is_forget, log_sig, z)


def _c_in_proj(x, mod, n_off, gain, w_in, wk_t, gate_b):
    N, S, D = x.shape
    W = MLSTM_WIDTH
    tm = min(ROW_TILE, S)
    row = lambda n, i: (n, i, 0)
    const = lambda n, i: (0, 0)
    return pl.pallas_call(
        _c_in_kernel,
        grid=(N, S // tm),
        in_specs=[
            pl.BlockSpec((None, tm, D), row),
            pl.BlockSpec((None, None, 6, D), lambda n, i: (1, n + n_off, 0, 0)),
            pl.BlockSpec((1, D), const),
            pl.BlockSpec((D, 4 * W + GATE_PAD), const),
            pl.BlockSpec((W, D), const),
            pl.BlockSpec((1, GATE_PAD), const),
        ],
        out_specs=[
            pl.BlockSpec((None, tm, W), row),
            pl.BlockSpec((None, W, tm), lambda n, i: (n, 0, i)),
            pl.BlockSpec((None, tm, W), row),
            pl.BlockSpec((None, tm, W), row),
            pl.BlockSpec((None, tm, GATE_PAD), row),
        ],
        out_shape=[
            jax.ShapeDtypeStruct((N, S, W), BF16),
            jax.ShapeDtypeStruct((N, W, S), BF16),
            jax.ShapeDtypeStruct((N, S, W), BF16),
            jax.ShapeDtypeStruct((N, S, W), BF16),
            jax.ShapeDtypeStruct((N, S, GATE_PAD), F32),
        ],
        compiler_params=_params("parallel", "parallel"),
        name="c_in_proj",
    )(x, mod, gain, w_in, wk_t, gate_b)


def _split3(x):
    hi = x.astype(BF16)
    r1 = x - hi.astype(F32)
    mid = r1.astype(BF16)
    lo = (r1 - mid.astype(F32)).astype(BF16)
    return hi, mid, lo


def _mlstm_kernel(qf_ref, kf_ref, vf_ref, gf_ref, qb_ref, kb_ref, vb_ref, gb_ref,
                  hf_ref, hb_ref, st_ref, m_ref):
    L = MLSTM_CHUNK
    dh = MLSTM_HEAD_DIM
    H = MLSTM_HEADS

    @pl.when(pl.program_id(1) == 0)
    def _():
        st_ref[...] = jnp.zeros_like(st_ref)
        m_ref[...] = jnp.zeros_like(m_ref)

    row = lax.broadcasted_iota(jnp.int32, (L, L), 0)
    col = lax.broadcasted_iota(jnp.int32, (L, L), 1)
    ones_blk = jnp.ones((L, dh), BF16)

    dirs = (
        (qf_ref, kf_ref, vf_ref, gf_ref, hf_ref, row >= col, L - 1),
        (qb_ref, kb_ref, vb_ref, gb_ref, hb_ref, row <= col, 0),
    )
    pre = []
    for d, (q_ref, kt_ref, v_ref, g_ref, h_ref, allowed, last) in enumerate(dirs):
        gates = g_ref[...]
        tri = allowed.astype(BF16)
        csum = sum(jnp.dot(tri, piece, preferred_element_type=F32) for piece in _split3(gates))
        gates_t = gates.T
        csum_t = csum.T
        lo = 2 * H * d
        ig_t = gates_t[lo:lo + H, :]
        b_t = csum_t[lo + H:lo + 2 * H, :]
        crow = ig_t - b_t
        b_last = b_t[:, last:last + 1]
        m_old = m_ref[d]
        wlog = b_last + crow
        m_new = jnp.maximum(b_last + m_old, jnp.max(wlog, axis=-1, keepdims=True))
        decay = jnp.exp(b_last + m_old - m_new)
        wrow = jnp.exp(wlog - m_new)
        m_ref[d] = m_new
        pre.append((csum, crow, m_old, decay, wrow))

    units = [(d, hd) for hd in range(H) for d in range(2)]

    stage_a = []
    for d, hd in units:
        q_ref, kt_ref, v_ref, g_ref, h_ref, allowed, last = dirs[d]
        csum, crow, m_old, decay, wrow = pre[d]
        sl = slice(hd * dh, (hd + 1) * dh)
        cf = 2 * H * d + H + hd
        q = q_ref[:, sl]
        kt = kt_ref[sl, :]
        bb = jnp.broadcast_to(csum[:, cf:cf + 1], (L, L))
        log_d = jnp.where(allowed, bb + crow[hd:hd + 1, :], NEG_INF)
        inter = bb + m_old[hd:hd + 1, :]
        mt = jnp.maximum(inter, jnp.max(log_d, axis=-1, keepdims=True))
        e_intra = jnp.exp(log_d - mt)
        e_inter = jnp.exp(inter - mt)
        sc = jnp.dot(q, kt, preferred_element_type=F32) * e_intra
        lhs = jnp.concatenate([sc.astype(BF16), (e_inter * q.astype(F32)).astype(BF16)], axis=1)
        stage_a.append((lhs, jnp.exp(-mt)))

    for (d, hd), (lhs, emt) in zip(units, stage_a):
        q_ref, kt_ref, v_ref, g_ref, h_ref, allowed, last = dirs[d]
        sl = slice(hd * dh, (hd + 1) * dh)
        v1 = jnp.concatenate([v_ref[:, sl], ones_blk], axis=1)
        rhs = jnp.concatenate([v1, st_ref[d, hd].astype(BF16)], axis=0)
        nd = jnp.dot(lhs, rhs, preferred_element_type=F32)
        den = jnp.maximum(jnp.abs(nd[:, dh:]), emt)
        h_ref[:, sl] = (nd[:, 0:dh] / den).astype(h_ref.dtype)

    for d, hd in units:
        q_ref, kt_ref, v_ref, g_ref, h_ref, allowed, last = dirs[d]
        csum, crow, m_old, decay, wrow = pre[d]
        sl = slice(hd * dh, (hd + 1) * dh)
        v1 = jnp.concatenate([v_ref[:, sl], ones_blk], axis=1)
        kw = (kt_ref[sl, :].astype(F32) * wrow[hd:hd + 1, :]).astype(BF16)
        dec = decay[hd:hd + 1, :]
        upd = jnp.dot(kw, v1, preferred_element_type=F32)
        st_ref[d, hd] = jnp.concatenate([dec, dec], axis=1) * st_ref[d, hd] + upd


def _mlstm_scan(q, kt, v, gates):
    N, S, W = q.shape
    L = MLSTM_CHUNK
    nc = S // L
    rows_f = lambda n, i: (n, i, 0)
    rows_b = lambda n, i: (n, nc - 1 - i, 0)
    cols_f = lambda n, i: (n, 0, i)
    cols_b = lambda n, i: (n, 0, nc - 1 - i)
    blk = lambda im: pl.BlockSpec((None, L, W), im)
    tblk = lambda im: pl.BlockSpec((None, W, L), im)
    gblk = lambda im: pl.BlockSpec((None, L, GATE_PAD), im)
    return pl.pallas_call(
        _mlstm_kernel,
        grid=(N, nc),
        in_specs=[blk(rows_f), tblk(cols_f), blk(rows_f), gblk(rows_f),
                  blk(rows_b), tblk(cols_b), blk(rows_b), gblk(rows_b)],
        out_specs=[blk(rows_f), blk(rows_b)],
        out_shape=[jax.ShapeDtypeStruct((N, S, W), BF16)] * 2,
        scratch_shapes=[
            pltpu.VMEM((2, MLSTM_HEADS, MLSTM_HEAD_DIM, 2 * MLSTM_HEAD_DIM), F32),
            pltpu.VMEM((2, MLSTM_HEADS, LANES), F32),
        ],
        compiler_params=_params("parallel", "arbitrary"),
        name="mlstm_scan",
    )(q, kt, v, gates, q, kt, v, gates)


def _trunk(x, mod, n_off, p):
    N, S, D = x.shape
    tables = _rope_tables(S)
    a, *qkv = _ab_in_proj(x, mod, n_off, p["norm_mix0"], p["ab_w_in"], tables)
    att_parts = [_dilated_attention(qkv[g], g) for g in range(N_PATTERNS)]
    x = _ab_tail(a, att_parts, x, mod, n_off, p["conv_w"], p["conv_b"], p["conv_ln_g"], p["conv_ln_b"],
                 p["ab_w_out"], p["norm_mlp0"], p["mlp_w1_0"], p["mlp_w2_0"])
    q1, kt1, v1, og, gates = _c_in_proj(x, mod, n_off, p["norm_mix1"], p["c_w_in"], p["c_wk_t"], p["c_gate_b"])
    hf, hb = _mlstm_scan(q1, kt1, v1, gates)
    return _c_tail(hf, hb, og, x, mod, n_off, p["c_head_norm"], p["c_w_out"], p["norm_mlp1"],
                   p["mlp_w1_1"], p["mlp_w2_1"], p["norm_final"])


def kernel(x_prompt, x_sample, c_prompt, c_sample, ada_w, ada_b, norm_mix, norm_mlp, ab_w_in, conv_w, conv_b, conv_ln_g, conv_ln_b, ab_w_out, c_w_in, c_gate_b, c_head_norm, c_w_out, mlp_w1, mlp_w2, norm_final):
    D = D_MODEL
    n_p, n_s = c_prompt.shape[0], c_sample.shape[0]
    rows = n_p + n_s
    rows_pad = -(-rows // 8) * 8
    c_all = jnp.concatenate([c_prompt, c_sample, jnp.zeros((rows_pad - rows, D), F32)], axis=0)
    mod = _modulation(c_all, ada_w, ada_b)

    W = MLSTM_WIDTH
    pad = GATE_PAD - N_GATES
    p = {
        "norm_mix0": norm_mix[0].reshape(1, D),
        "norm_mix1": norm_mix[1].reshape(1, D),
        "norm_mlp0": norm_mlp[0].reshape(1, D),
        "norm_mlp1": norm_mlp[1].reshape(1, D),
        "ab_w_in": ab_w_in[0].astype(BF16),
        "conv_w": conv_w[0],
        "conv_b": conv_b[0].reshape(1, CONV_CH),
        "conv_ln_g": conv_ln_g[0].reshape(1, CONV_CH),
        "conv_ln_b": conv_ln_b[0].reshape(1, CONV_CH),
        "ab_w_out": ab_w_out[0].astype(BF16),
        "c_w_in": jnp.pad(c_w_in[0], ((0, 0), (0, pad))).astype(BF16),
        "c_wk_t": c_w_in[0][:, W:2 * W].T.astype(BF16),
        "c_gate_b": jnp.pad(c_gate_b[0], (0, pad)).reshape(1, GATE_PAD),
        "c_head_norm": c_head_norm[0].reshape(1, W),
        "c_w_out": c_w_out[0].astype(BF16),
        "mlp_w1_0": mlp_w1[0].astype(BF16),
        "mlp_w2_0": mlp_w2[0].astype(BF16),
        "mlp_w1_1": mlp_w1[1].astype(BF16),
        "mlp_w2_1": mlp_w2[1].astype(BF16),
        "norm_final": norm_final.reshape(1, D),
    }
    y_prompt = _trunk(x_prompt, mod, 0, p)
    y_sample = _trunk(x_sample, mod, n_p, p)
    return (y_prompt, y_sample)
```

```python
import functools

import jax
import jax.numpy as jnp
from jax import lax
from jax.experimental import pallas as pl
from jax.experimental.pallas import tpu as pltpu

D_MODEL = 1024
DEPTH = 2
CONV_CH = 512
CONV_WIDTH = 31
CONV_HALO = 16
ATT_HEADS = 8
ATT_HEAD_DIM = 64
ATT_WIDTH = ATT_HEADS * ATT_HEAD_DIM
DILATED_PATTERNS = ((128, 1), (512, 4), (2048, 16))
N_PATTERNS = len(DILATED_PATTERNS)
ATT_RADIUS = 64
ROPE_THETA = 500000.0
ROPE_DIM = ATT_HEAD_DIM // 4
ROPE_HALF = ROPE_DIM // 2
AB_IN = 2 * CONV_CH + N_PATTERNS * 3 * ATT_WIDTH
QKV_W = N_PATTERNS * 3 * ATT_WIDTH
MLSTM_HEADS = 8
MLSTM_WIDTH = D_MODEL
MLSTM_HEAD_DIM = MLSTM_WIDTH // MLSTM_HEADS
MLSTM_CHUNK = 128
N_GATES = 4 * MLSTM_HEADS
GATE_PAD = 128
D_FF = 4 * D_MODEL
EPS = 1e-6
NEG_INF = -1e30
LOG2_E = 1.4426950408889634
LN_2 = 0.6931471805599453

LANES = 128
VMEM_LIMIT = 56 * 1024 * 1024
ROW_TILE = 512
ATT_TQ = 128
ATT_TILES_PER_STEP = 8
CONV_ROWS = 32
CONV_CHAINS = 4
NORM_ROWS = 128
MLP_FF_TILE = 1024

F32 = jnp.float32
BF16 = jnp.bfloat16


def _params(*sem):
    return pltpu.CompilerParams(dimension_semantics=sem, vmem_limit_bytes=VMEM_LIMIT)


def _norm_mod(x, gain, scale, shift):
    y = x * lax.rsqrt(jnp.mean(x * x, axis=-1, keepdims=True) + EPS)
    return (y * gain) * (1.0 + scale) + shift


def _mod_kernel(c_ref, w_ref, b_ref, o_ref):
    c = c_ref[...]
    s = c * jax.nn.sigmoid(c)
    o_ref[...] = jnp.dot(s, w_ref[...], preferred_element_type=F32,
                         precision=lax.Precision.HIGHEST) + b_ref[...]


def _modulation(c_all, ada_w, ada_b):
    R = c_all.shape[0]
    D = D_MODEL
    out = pl.pallas_call(
        _mod_kernel,
        grid=(DEPTH, 6),
        in_specs=[
            pl.BlockSpec((R, D), lambda l, j: (0, 0)),
            pl.BlockSpec((None, D, D), lambda l, j: (l, 0, j)),
            pl.BlockSpec((None, 1, D), lambda l, j: (l, 0, j)),
        ],
        out_specs=pl.BlockSpec((None, R, D), lambda l, j: (l, 0, j)),
        out_shape=jax.ShapeDtypeStruct((DEPTH, R, 6 * D), F32),
        compiler_params=_params("arbitrary", "arbitrary"),
        name="adaln_mod",
    )(c_all, ada_w, ada_b.reshape(DEPTH, 1, 6 * D))
    return out.reshape(DEPTH, R, 6, D)


def _rope_tables(S):
    inv_freq = jnp.power(ROPE_THETA, -jnp.arange(ROPE_HALF, dtype=F32) / ROPE_HALF)
    ang = jnp.arange(S).astype(F32)[:, None] * inv_freq[None, :]
    cos, sin = jnp.cos(ang), jnp.sin(ang)
    ones = jnp.ones((S, ATT_HEAD_DIM - ROPE_DIM), F32)
    zeros = jnp.zeros((S, ATT_HEAD_DIM - ROPE_DIM), F32)
    zh = jnp.zeros((S, ROPE_HALF), F32)
    cos_t = jnp.concatenate([cos, cos, ones], axis=1)
    sin_up = jnp.concatenate([-sin, zh, zeros], axis=1)
    sin_dn = jnp.concatenate([zh, sin, zeros], axis=1)
    rep = LANES // ATT_HEAD_DIM
    return tuple(jnp.tile(t, (1, rep)) for t in (cos_t, sin_up, sin_dn))


def _ab_in_kernel(x_ref, mod_ref, gain_ref, w_ref, *rest, tm):
    tab_refs = rest[:3]
    a_ref = rest[3]
    out_refs = rest[4:4 + N_PATTERNS]
    h_ref = rest[-1]
    hf = _norm_mod(x_ref[...], gain_ref[...], mod_ref[1:2, :], mod_ref[0:1, :])
    n_slab = hf.shape[1] // LANES
    for c in range(n_slab):
        h_ref[c] = hf[:, c * LANES:(c + 1) * LANES]
    h = hf.astype(BF16)

    def seg(lhs, j):
        return jnp.dot(lhs, w_ref[:, j * ATT_WIDTH:(j + 1) * ATT_WIDTH], preferred_element_type=F32)

    a_ref[...] = (seg(h, 0) * jax.nn.sigmoid(seg(h, 1))).astype(a_ref.dtype)

    def rope(p, tabs, scale):
        cos_t, sin_up, sin_dn = tabs
        parts = []
        for j in range(ATT_WIDTH // LANES):
            t = p[:, j * LANES:(j + 1) * LANES]
            up = pltpu.roll(t, LANES - ROPE_HALF, 1)
            dn = pltpu.roll(t, ROPE_HALF, 1)
            r = t * cos_t + up * sin_up + dn * sin_dn
            if scale != 1.0:
                r = r * scale
            parts.append(r.astype(BF16))
        return jnp.concatenate(parts, axis=1)

    for g, (_, dil) in enumerate(DILATED_PATTERNS):
        lt = tm // dil
        if dil == 1:
            hg = h
            tabs = tuple(t[...] for t in tab_refs)
        else:
            hg = jnp.concatenate(
                [jnp.concatenate([h_ref[c, pl.ds(r, lt, stride=dil), :] for c in range(n_slab)], axis=1)
                 for r in range(dil)], axis=0).astype(BF16)
            tabs = tuple(jnp.concatenate([t[pl.ds(r, lt, stride=dil), :] for r in range(dil)], axis=0)
                         for t in tab_refs)
        q = rope(seg(hg, 2 + 3 * g), tabs, ATT_HEAD_DIM ** -0.5 * LOG2_E)
        k = rope(seg(hg, 3 + 3 * g), tabs, 1.0)
        v = seg(hg, 4 + 3 * g).astype(BF16)
        for c, val in enumerate((q, k, v)):
            for r in range(dil):
                out_refs[g][r, :, c * ATT_WIDTH:(c + 1) * ATT_WIDTH] = val[r * lt:(r + 1) * lt, :]


def _ab_in_proj(x, mod, n_off, gain, w_in, tables):
    N, S, D = x.shape
    tm = min(ROW_TILE, S)
    tab_spec = pl.BlockSpec((tm, LANES), lambda n, i: (i, 0))
    out_specs = [pl.BlockSpec((None, tm, CONV_CH), lambda n, i: (n, i, 0))]
    out_shape = [jax.ShapeDtypeStruct((N, S, CONV_CH), BF16)]
    for _, dil in DILATED_PATTERNS:
        out_specs.append(pl.BlockSpec((None, dil, tm // dil, 3 * ATT_WIDTH), lambda n, i: (n, 0, i, 0)))
        out_shape.append(jax.ShapeDtypeStruct((N, dil, S // dil, 3 * ATT_WIDTH), BF16))
    return pl.pallas_call(
        functools.partial(_ab_in_kernel, tm=tm),
        grid=(N, S // tm),
        in_specs=[
            pl.BlockSpec((None, tm, D), lambda n, i: (n, i, 0)),
            pl.BlockSpec((None, None, 6, D), lambda n, i: (0, n + n_off, 0, 0)),
            pl.BlockSpec((1, D), lambda n, i: (0, 0)),
            pl.BlockSpec((D, AB_IN), lambda n, i: (0, 0)),
        ] + [tab_spec] * 3,
        out_specs=out_specs,
        out_shape=out_shape,
        scratch_shapes=[pltpu.VMEM((D // LANES, tm, LANES), F32)],
        compiler_params=_params("parallel", "parallel"),
        name="ab_in_proj",
    )(x, mod, gain, w_in, *tables)


def _attn_kernel(q_ref, kp_ref, kc_ref, kn_ref, vp_ref, vc_ref, vn_ref, o_ref, l_ref, *, tq, n_sub, n_seq, n_blocks):
    i = pl.program_id(2)
    R = ATT_RADIUS
    bq = n_sub * tq
    tk = tq + 2 * R
    row = lax.broadcasted_iota(jnp.int32, (tq, tk), 0)
    col = lax.broadcasted_iota(jnp.int32, (tq, tk), 1)
    rel = col - row
    band = (rel >= 0) & (rel <= 2 * R)
    masks = []
    for t in range(n_sub):
        mask = band
        if t == 0:
            mask = mask & (col >= jnp.where(i == 0, R, 0))
        if t == n_sub - 1:
            mask = mask & (col < jnp.where(i == n_blocks - 1, tq + R, tk))
        masks.append(mask)
    lane = lax.broadcasted_iota(jnp.int32, (1, LANES), 1)
    first_head = lane < ATT_HEAD_DIM

    def window(p_ref, c_ref, n_ref, r, t, sl):
        lo, hi = t * tq - R, (t + 1) * tq + R
        parts = []
        if lo < 0:
            parts.append(p_ref[r, :, sl])
        parts.append(c_ref[r, max(lo, 0):min(hi, bq), sl])
        if hi > bq:
            parts.append(n_ref[r, :, sl])
        return jnp.concatenate(parts, axis=0)

    for r in range(n_seq):
        for t in range(n_sub):
            rows = slice(t * tq, (t + 1) * tq)
            lse_all = jnp.zeros((tq, LANES), F32)
            for j in range(ATT_WIDTH // LANES):
                sl = slice(j * LANES, (j + 1) * LANES)
                q = q_ref[r, rows, sl]
                kw = window(kp_ref, kc_ref, kn_ref, r, t, sl)
                vw = window(vp_ref, vc_ref, vn_ref, r, t, sl)
                v1 = jnp.concatenate([vw, jnp.ones_like(vw)], axis=1)
                o_pair = None
                for hh in range(2):
                    sel = first_head if hh == 0 else jnp.logical_not(first_head)
                    qm = jnp.where(sel, q, jnp.zeros_like(q))
                    s = lax.dot_general(qm, kw, (((1,), (1,)), ((), ())), preferred_element_type=F32)
                    s = jnp.where(masks[t], s, NEG_INF)
                    m = jnp.max(s, axis=-1, keepdims=True)
                    p = jnp.exp2((s - m).astype(BF16))
                    ol = jnp.dot(p, v1, preferred_element_type=F32)
                    l = ol[:, LANES:]
                    o_h = ol[:, 0:LANES] / l
                    o_pair = o_h if hh == 0 else jnp.where(first_head, o_pair, o_h)
                    lse_all = jnp.where(lane == 2 * j + hh, (m + jnp.log2(l)) * LN_2, lse_all)
                o_ref[r, rows, sl] = o_pair.astype(o_ref.dtype)
            l_ref[r, rows, :] = lse_all


def _dilated_attention(qkv_g, g):
    N, dil, L, _ = qkv_g.shape
    tq = min(ATT_TQ, L)
    n_sub = min(ATT_TILES_PER_STEP, L // tq)
    n_seq = min(ATT_TILES_PER_STEP // n_sub, dil)
    bq = n_sub * tq
    nb = L // bq
    hb = bq // ATT_RADIUS
    n_halo = L // ATT_RADIUS

    def cur(c):
        return pl.BlockSpec((None, n_seq, bq, ATT_WIDTH), lambda n, r, i: (n, r, i, c))

    def before(c):
        return pl.BlockSpec((None, n_seq, ATT_RADIUS, ATT_WIDTH),
                            lambda n, r, i: (n, r, jnp.maximum(i * hb - 1, 0), c))

    def after(c):
        return pl.BlockSpec((None, n_seq, ATT_RADIUS, ATT_WIDTH),
                            lambda n, r, i: (n, r, jnp.minimum((i + 1) * hb, n_halo - 1), c))

    return pl.pallas_call(
        functools.partial(_attn_kernel, tq=tq, n_sub=n_sub, n_seq=n_seq, n_blocks=nb),
        grid=(N, dil // n_seq, nb),
        in_specs=[cur(0), before(1), cur(1), after(1), before(2), cur(2), after(2)],
        out_specs=[
            pl.BlockSpec((None, n_seq, bq, ATT_WIDTH), lambda n, r, i: (n, r, i, 0)),
            pl.BlockSpec((None, n_seq, bq, LANES), lambda n, r, i: (n, r, i, 0)),
        ],
        out_shape=[
            jax.ShapeDtypeStruct((N, dil, L, ATT_WIDTH), BF16),
            jax.ShapeDtypeStruct((N, dil, L, LANES), F32),
        ],
        compiler_params=_params("parallel", "parallel", "parallel"),
        name=f"dilated_attn_{g}",
    )(*([qkv_g] * 7))


def _squared_relu_mlp(h, w1_ref, w2_ref):
    acc = None
    for c in range(D_FF // MLP_FF_TILE):
        sl = slice(c * MLP_FF_TILE, (c + 1) * MLP_FF_TILE)
        u = jnp.maximum(jnp.dot(h, w1_ref[:, sl], preferred_element_type=F32), 0.0)
        part = jnp.dot((u * u).astype(BF16), w2_ref[sl, :], preferred_element_type=F32)
        acc = part if acc is None else acc + part
    return acc


def _ab_tail_kernel(ap_ref, ac_ref, an_ref, o0_ref, l0_ref, o1_ref, l1_ref, o2_ref, l2_ref, x_ref, mod_ref,
                    cw_ref, cb_ref, lg_ref, lb_ref, wo_ref, gain_ref, w1_ref, w2_ref, o_ref,
                    sh_ref, conv_ref, on_ref, ln_ref, act_ref, *, tm, n_tiles):
    i = pl.program_id(1)
    H = CONV_HALO
    SUB = 8

    sh_ref[0, 0:H, :] = jnp.where(i > 0, ap_ref[...].astype(F32), 0.0)
    sh_ref[0, H:H + tm, :] = ac_ref[...].astype(F32)
    sh_ref[0, H + tm:H + tm + H, :] = jnp.where(i < n_tiles - 1, an_ref[...].astype(F32), 0.0)
    span = tm + 2 * H - SUB
    for s in range(1, SUB):
        sh_ref[s, 0:span, :] = sh_ref[0, s:s + span, :]

    off = H - CONV_WIDTH // 2
    for c in range(CONV_CH // LANES):
        lanes = slice(c * LANES, (c + 1) * LANES)
        taps = [jnp.broadcast_to(cw_ref[j:j + 1, lanes], (CONV_ROWS, LANES)) for j in range(CONV_WIDTH)]
        bias = jnp.broadcast_to(cb_ref[:, lanes], (CONV_ROWS, LANES))
        for k in range(tm // CONV_ROWS):
            r0 = k * CONV_ROWS
            accs = [bias] + [None] * (CONV_CHAINS - 1)
            for j in range(CONV_WIDTH):
                s, blk = (off + j) % SUB, (off + j) // SUB
                term = taps[j] * sh_ref[s, pl.ds(r0 + SUB * blk, CONV_ROWS), lanes]
                a = accs[j % CONV_CHAINS]
                accs[j % CONV_CHAINS] = term if a is None else a + term
            while len(accs) > 1:
                accs = [accs[u] + accs[u + 1] for u in range(0, len(accs), 2)]
            conv_ref[pl.ds(r0, CONV_ROWS), lanes] = accs[0]

    def norm_chunk(k, carry):
        r0 = pl.multiple_of(k * NORM_ROWS, NORM_ROWS)
        acc = conv_ref[pl.ds(r0, NORM_ROWS), :]
        mu = jnp.mean(acc, axis=-1, keepdims=True)
        xc = acc - mu
        y = xc * lax.rsqrt(jnp.mean(xc * xc, axis=-1, keepdims=True) + EPS)
        y = y * lg_ref[...] + lb_ref[...]
        act_ref[pl.ds(r0, NORM_ROWS), :] = (y * jax.nn.sigmoid(y)).astype(act_ref.dtype)
        return carry

    lax.fori_loop(0, tm // NORM_ROWS, norm_chunk, 0)

    for g, (o_g, l_g) in enumerate(((o0_ref, l0_ref), (o1_ref, l1_ref), (o2_ref, l2_ref))):
        dil = DILATED_PATTERNS[g][1]
        lt = tm // dil
        for r in range(dil):
            rows = pl.ds(r, lt, stride=dil) if dil > 1 else pl.ds(0, tm)
            o_r = o_g[r].astype(F32)
            for c in range(ATT_WIDTH // LANES):
                on_ref[g, c, rows, :] = o_r[:, c * LANES:(c + 1) * LANES]
            ln_ref[g, rows, :] = l_g[r]
    lses = [ln_ref[g] for g in range(N_PATTERNS)]
    mx = functools.reduce(jnp.maximum, lses)
    es = [jnp.exp(l - mx) for l in lses]
    tot = functools.reduce(lambda a, b: a + b, es)
    erow = lax.broadcasted_iota(jnp.int32, (LANES, ATT_WIDTH), 0)
    ecol = lax.broadcasted_iota(jnp.int32, (LANES, ATT_WIDTH), 1)
    expand = (ecol // ATT_HEAD_DIM == erow).astype(BF16)
    att = None
    for g in range(N_PATTERNS):
        wgt = es[g] / tot
        hi = wgt.astype(BF16)
        lo = (wgt - hi.astype(F32)).astype(BF16)
        wide = jnp.dot(hi, expand, preferred_element_type=F32) + jnp.dot(lo, expand, preferred_element_type=F32)
        term = wide * jnp.concatenate([on_ref[g, c] for c in range(ATT_WIDTH // LANES)], axis=1)
        att = term if att is None else att + term

    mix = jnp.dot(act_ref[...], wo_ref[0:CONV_CH, :], preferred_element_type=F32)
    mix = mix + jnp.dot(att.astype(BF16), wo_ref[CONV_CH:, :], preferred_element_type=F32)
    x1 = x_ref[...] + mod_ref[2:3, :] * mix
    h2 = _norm_mod(x1, gain_ref[...], mod_ref[4:5, :], mod_ref[3:4, :]).astype(BF16)
    o_ref[...] = x1 + mod_ref[5:6, :] * _squared_relu_mlp(h2, w1_ref, w2_ref)


def _ab_tail(a, att_parts, x, mod, n_off, conv_w, conv_b, ln_g, ln_b, w_out, gain, w1, w2):
    N, S, D = x.shape
    tm = min(ROW_TILE, S)
    nt = S // tm
    hb = tm // CONV_HALO
    n_halo = S // CONV_HALO
    row = lambda n, i: (n, i, 0)
    const = lambda n, i: (0, 0)
    once = pl.Buffered(1)
    att_specs, att_args = [], []
    for (_, dil), (o_g, l_g) in zip(DILATED_PATTERNS, att_parts):
        att_specs.append(pl.BlockSpec((None, dil, tm // dil, ATT_WIDTH), lambda n, i: (n, 0, i, 0)))
        att_specs.append(pl.BlockSpec((None, dil, tm // dil, LANES), lambda n, i: (n, 0, i, 0)))
        att_args += [o_g, l_g]
    return pl.pallas_call(
        functools.partial(_ab_tail_kernel, tm=tm, n_tiles=nt),
        grid=(N, nt),
        in_specs=[
            pl.BlockSpec((None, CONV_HALO, CONV_CH), lambda n, i: (n, jnp.maximum(i * hb - 1, 0), 0)),
            pl.BlockSpec((None, tm, CONV_CH), row),
            pl.BlockSpec((None, CONV_HALO, CONV_CH), lambda n, i: (n, jnp.minimum((i + 1) * hb, n_halo - 1), 0)),
        ] + att_specs + [
            pl.BlockSpec((None, tm, D), row),
            pl.BlockSpec((None, None, 6, D), lambda n, i: (0, n + n_off, 0, 0)),
            pl.BlockSpec((CONV_WIDTH, CONV_CH), const),
            pl.BlockSpec((1, CONV_CH), const),
            pl.BlockSpec((1, CONV_CH), const),
            pl.BlockSpec((1, CONV_CH), const),
            pl.BlockSpec((D, D), const, pipeline_mode=once),
            pl.BlockSpec((1, D), const),
            pl.BlockSpec((D, D_FF), const, pipeline_mode=once),
            pl.BlockSpec((D_FF, D), const, pipeline_mode=once),
        ],
        out_specs=pl.BlockSpec((None, tm, D), row),
        out_shape=jax.ShapeDtypeStruct((N, S, D), F32),
        scratch_shapes=[
            pltpu.VMEM((8, tm + 2 * CONV_HALO, CONV_CH), F32),
            pltpu.VMEM((tm, CONV_CH), F32),
            pltpu.VMEM((N_PATTERNS, ATT_WIDTH // LANES, tm, LANES), F32),
            pltpu.VMEM((N_PATTERNS, tm, LANES), F32),
            pltpu.VMEM((tm, CONV_CH), BF16),
        ],
        compiler_params=_params("parallel", "parallel"),
        name="ab_tail",
    )(a, a, a, *att_args, x, mod, conv_w, conv_b, ln_g, ln_b, w_out, gain, w1, w2)


def _c_tail_kernel(hf_ref, hb_ref, og_ref, x_ref, mod_ref, hn_ref, wo_ref, gain_ref, w1_ref, w2_ref, fg_ref, o_ref):
    dh = MLSTM_HEAD_DIM
    parts = []
    for hd in range(MLSTM_HEADS):
        sl = slice(hd * dh, (hd + 1) * dh)
        ht = hf_ref[:, sl].astype(F32) + hb_ref[:, sl].astype(F32)
        ht = ht * lax.rsqrt(jnp.mean(ht * ht, axis=-1, keepdims=True) + EPS)
        parts.append((og_ref[:, sl].astype(F32) * (ht * hn_ref[:, sl])).astype(BF16))
    z = jnp.concatenate(parts, axis=1)
    mix = jnp.dot(z, wo_ref[...], preferred_element_type=F32)
    x1 = x_ref[...] + mod_ref[2:3, :] * mix
    h2 = _norm_mod(x1, gain_ref[...], mod_ref[4:5, :], mod_ref[3:4, :]).astype(BF16)
    y = x1 + mod_ref[5:6, :] * _squared_relu_mlp(h2, w1_ref, w2_ref)
    o_ref[...] = y * lax.rsqrt(jnp.mean(y * y, axis=-1, keepdims=True) + EPS) * fg_ref[...]


def _c_tail(hf, hb, og, x, mod, n_off, head_norm, w_out, gain, w1, w2, final_gain):
    N, S, D = x.shape
    W = MLSTM_WIDTH
    tm = min(ROW_TILE, S)
    row = lambda n, i: (n, i, 0)
    const = lambda n, i: (0, 0)
    once = pl.Buffered(1)
    return pl.pallas_call(
        _c_tail_kernel,
        grid=(N, S // tm),
        in_specs=[
            pl.BlockSpec((None, tm, W), row),
            pl.BlockSpec((None, tm, W), row),
            pl.BlockSpec((None, tm, W), row),
            pl.BlockSpec((None, tm, D), row),
            pl.BlockSpec((None, None, 6, D), lambda n, i: (1, n + n_off, 0, 0)),
            pl.BlockSpec((1, W), const),
            pl.BlockSpec((W, D), const, pipeline_mode=once),
            pl.BlockSpec((1, D), const),
            pl.BlockSpec((D, D_FF), const, pipeline_mode=once),
            pl.BlockSpec((D_FF, D), const, pipeline_mode=once),
            pl.BlockSpec((1, D), const),
        ],
        out_specs=pl.BlockSpec((None, tm, D), row),
        out_shape=jax.ShapeDtypeStruct((N, S, D), F32),
        compiler_params=_params("parallel", "parallel"),
        name="c_tail",
    )(hf, hb, og, x, mod, head_norm, w_out, gain, w1, w2, final_gain)


def _c_in_kernel(x_ref, mod_ref, gain_ref, w_ref, wkt_ref, gb_ref, q_ref, kt_ref, v_ref, og_ref, gate_ref):
    W = MLSTM_WIDTH
    x = x_ref[...]
    h = _norm_mod(x, gain_ref[...], mod_ref[1:2, :], mod_ref[0:1, :]).astype(BF16)

    def seg(lo, width):
        return jnp.dot(h, w_ref[:, lo:lo + width], preferred_element_type=F32)

    q_ref[...] = seg(0, W).astype(q_ref.dtype)
    kt = lax.dot_general(wkt_ref[...], h, (((1,), (1,)), ((), ())), preferred_element_type=F32)
    kt_ref[...] = (kt * (MLSTM_HEAD_DIM ** -0.5)).astype(kt_ref.dtype)
    v_ref[...] = seg(2 * W, W).astype(v_ref.dtype)
    og_ref[...] = jax.nn.sigmoid(seg(3 * W, W)).astype(og_ref.dtype)
    z = seg(4 * W, GATE_PAD) + gb_ref[...]
    lane = lax.broadcasted_iota(jnp.int32, (1, GATE_PAD), 1)
    is_forget = (lane % (2 * MLSTM_HEADS)) >= MLSTM_HEADS
    log_sig = jnp.minimum(z, 0.0) - jnp.log1p(jnp.exp(-jnp.abs(z)))
    gate_ref[...] = jnp.where(is_forget, log_sig, z)


def _c_in_proj(x, mod, n_off, gain, w_in, wk_t, gate_b):
    N, S, D = x.shape
    W = MLSTM_WIDTH
    tm = min(ROW_TILE, S)
    row = lambda n, i: (n, i, 0)
    const = lambda n, i: (0, 0)
    return pl.pallas_call(
        _c_in_kernel,
        grid=(N, S // tm),
        in_specs=[
            pl.BlockSpec((None, tm, D), row),
            pl.BlockSpec((None, None, 6, D), lambda n, i: (1, n + n_off, 0, 0)),
            pl.BlockSpec((1, D), const),
            pl.BlockSpec((D, 4 * W + GATE_PAD), const),
            pl.BlockSpec((W, D), const),
            pl.BlockSpec((1, GATE_PAD), const),
        ],
        out_specs=[
            pl.BlockSpec((None, tm, W), row),
            pl.BlockSpec((None, W, tm), lambda n, i: (n, 0, i)),
            pl.BlockSpec((None, tm, W), row),
            pl.BlockSpec((None, tm, W), row),
            pl.BlockSpec((None, tm, GATE_PAD), row),
        ],
        out_shape=[
            jax.ShapeDtypeStruct((N, S, W), BF16),
            jax.ShapeDtypeStruct((N, W, S), BF16),
            jax.ShapeDtypeStruct((N, S, W), BF16),
            jax.ShapeDtypeStruct((N, S, W), BF16),
            jax.ShapeDtypeStruct((N, S, GATE_PAD), F32),
        ],
        compiler_params=_params("parallel", "parallel"),
        name="c_in_proj",
    )(x, mod, gain, w_in, wk_t, gate_b)


def _split3(x):
    hi = x.astype(BF16)
    r1 = x - hi.astype(F32)
    mid = r1.astype(BF16)
    lo = (r1 - mid.astype(F32)).astype(BF16)
    return hi, mid, lo


def _mlstm_kernel(qf_ref, kf_ref, vf_ref, gf_ref, qb_ref, kb_ref, vb_ref, gb_ref,
                  hf_ref, hb_ref, st_ref, m_ref):
    L = MLSTM_CHUNK
    dh = MLSTM_HEAD_DIM
    H = MLSTM_HEADS

    @pl.when(pl.program_id(1) == 0)
    def _():
        st_ref[...] = jnp.zeros_like(st_ref)
        m_ref[...] = jnp.zeros_like(m_ref)

    row = lax.broadcasted_iota(jnp.int32, (L, L), 0)
    col = lax.broadcasted_iota(jnp.int32, (L, L), 1)
    ones_blk = jnp.ones((L, dh), BF16)

    dirs = (
        (qf_ref, kf_ref, vf_ref, gf_ref, hf_ref, row >= col, L - 1),
        (qb_ref, kb_ref, vb_ref, gb_ref, hb_ref, row <= col, 0),
    )
    pre = []
    for d, (q_ref, kt_ref, v_ref, g_ref, h_ref, allowed, last) in enumerate(dirs):
        gates = g_ref[...]
        tri = allowed.astype(BF16)
        csum = sum(jnp.dot(tri, piece, preferred_element_type=F32) for piece in _split3(gates))
        gates_t = gates.T
        csum_t = csum.T
        lo = 2 * H * d
        ig_t = gates_t[lo:lo + H, :]
        b_t = csum_t[lo + H:lo + 2 * H, :]
        crow = ig_t - b_t
        b_last = b_t[:, last:last + 1]
        m_old = m_ref[d]
        wlog = b_last + crow
        m_new = jnp.maximum(b_last + m_old, jnp.max(wlog, axis=-1, keepdims=True))
        decay = jnp.exp(b_last + m_old - m_new)
        wrow = jnp.exp(wlog - m_new)
        m_ref[d] = m_new
        pre.append((csum, crow, m_old, decay, wrow))

    units = [(d, hd) for hd in range(H) for d in range(2)]

    stage_a = []
    for d, hd in units:
        q_ref, kt_ref, v_ref, g_ref, h_ref, allowed, last = dirs[d]
        csum, crow, m_old, decay, wrow = pre[d]
        sl = slice(hd * dh, (hd + 1) * dh)
        cf = 2 * H * d + H + hd
        q = q_ref[:, sl]
        kt = kt_ref[sl, :]
        bb = jnp.broadcast_to(csum[:, cf:cf + 1], (L, L))
        log_d = jnp.where(allowed, bb + crow[hd:hd + 1, :], NEG_INF)
        inter = bb + m_old[hd:hd + 1, :]
        mt = jnp.maximum(inter, jnp.max(log_d, axis=-1, keepdims=True))
        e_intra = jnp.exp(log_d - mt)
        e_inter = jnp.exp(inter - mt)
        sc = jnp.dot(q, kt, preferred_element_type=F32) * e_intra
        lhs = jnp.concatenate([sc.astype(BF16), (e_inter * q.astype(F32)).astype(BF16)], axis=1)
        stage_a.append((lhs, jnp.exp(-mt)))

    for (d, hd), (lhs, emt) in zip(units, stage_a):
        q_ref, kt_ref, v_ref, g_ref, h_ref, allowed, last = dirs[d]
        sl = slice(hd * dh, (hd + 1) * dh)
        v1 = jnp.concatenate([v_ref[:, sl], ones_blk], axis=1)
        rhs = jnp.concatenate([v1, st_ref[d, hd].astype(BF16)], axis=0)
        nd = jnp.dot(lhs, rhs, preferred_element_type=F32)
        den = jnp.maximum(jnp.abs(nd[:, dh:]), emt)
        h_ref[:, sl] = (nd[:, 0:dh] / den).astype(h_ref.dtype)

    for d, hd in units:
        q_ref, kt_ref, v_ref, g_ref, h_ref, allowed, last = dirs[d]
        csum, crow, m_old, decay, wrow = pre[d]
        sl = slice(hd * dh, (hd + 1) * dh)
        v1 = jnp.concatenate([v_ref[:, sl], ones_blk], axis=1)
        kw = (kt_ref[sl, :].astype(F32) * wrow[hd:hd + 1, :]).astype(BF16)
        dec = decay[hd:hd + 1, :]
        upd = jnp.dot(kw, v1, preferred_element_type=F32)
        st_ref[d, hd] = jnp.concatenate([dec, dec], axis=1) * st_ref[d, hd] + upd


def _mlstm_scan(q, kt, v, gates):
    N, S, W = q.shape
    L = MLSTM_CHUNK
    nc = S // L
    rows_f = lambda n, i: (n, i, 0)
    rows_b = lambda n, i: (n, nc - 1 - i, 0)
    cols_f = lambda n, i: (n, 0, i)
    cols_b = lambda n, i: (n, 0, nc - 1 - i)
    blk = lambda im: pl.BlockSpec((None, L, W), im)
    tblk = lambda im: pl.BlockSpec((None, W, L), im)
    gblk = lambda im: pl.BlockSpec((None, L, GATE_PAD), im)
    return pl.pallas_call(
        _mlstm_kernel,
        grid=(N, nc),
        in_specs=[blk(rows_f), tblk(cols_f), blk(rows_f), gblk(rows_f),
                  blk(rows_b), tblk(cols_b), blk(rows_b), gblk(rows_b)],
        out_specs=[blk(rows_f), blk(rows_b)],
        out_shape=[jax.ShapeDtypeStruct((N, S, W), BF16)] * 2,
        scratch_shapes=[
            pltpu.VMEM((2, MLSTM_HEADS, MLSTM_HEAD_DIM, 2 * MLSTM_HEAD_DIM), F32),
            pltpu.VMEM((2, MLSTM_HEADS, LANES), F32),
        ],
        compiler_params=_params("parallel", "arbitrary"),
        name="mlstm_scan",
    )(q, kt, v, gates, q, kt, v, gates)


def _trunk(x, mod, n_off, p):
    N, S, D = x.shape
    tables = _rope_tables(S)
    a, *qkv = _ab_in_proj(x, mod, n_off, p["norm_mix0"], p["ab_w_in"], tables)
    att_parts = [_dilated_attention(qkv[g], g) for g in range(N_PATTERNS)]
    x = _ab_tail(a, att_parts, x, mod, n_off, p["conv_w"], p["conv_b"], p["conv_ln_g"], p["conv_ln_b"],
                 p["ab_w_out"], p["norm_mlp0"], p["mlp_w1_0"], p["mlp_w2_0"])
    q1, kt1, v1, og, gates = _c_in_proj(x, mod, n_off, p["norm_mix1"], p["c_w_in"], p["c_wk_t"], p["c_gate_b"])
    hf, hb = _mlstm_scan(q1, kt1, v1, gates)
    return _c_tail(hf, hb, og, x, mod, n_off, p["c_head_norm"], p["c_w_out"], p["norm_mlp1"],
                   p["mlp_w1_1"], p["mlp_w2_1"], p["norm_final"])


def kernel(x_prompt, x_sample, c_prompt, c_sample, ada_w, ada_b, norm_mix, norm_mlp, ab_w_in, conv_w, conv_b, conv_ln_g, conv_ln_b, ab_w_out, c_w_in, c_gate_b, c_head_norm, c_w_out, mlp_w1, mlp_w2, norm_final):
    D = D_MODEL
    n_p, n_s = c_prompt.shape[0], c_sample.shape[0]
    rows = n_p + n_s
    rows_pad = -(-rows // 8) * 8
    c_all = jnp.concatenate([c_prompt, c_sample, jnp.zeros((rows_pad - rows, D), F32)], axis=0)
    mod = _modulation(c_all, ada_w, ada_b)

    W = MLSTM_WIDTH
    pad = GATE_PAD - N_GATES
    p = {
        "norm_mix0": norm_mix[0].reshape(1, D),
        "norm_mix1": norm_mix[1].reshape(1, D),
        "norm_mlp0": norm_mlp[0].reshape(1, D),
        "norm_mlp1": norm_mlp[1].reshape(1, D),
        "ab_w_in": ab_w_in[0].astype(BF16),
        "conv_w": conv_w[0],
        "conv_b": conv_b[0].reshape(1, CONV_CH),
        "conv_ln_g": conv_ln_g[0].reshape(1, CONV_CH),
        "conv_ln_b": conv_ln_b[0].reshape(1, CONV_CH),
        "ab_w_out": ab_w_out[0].astype(BF16),
        "c_w_in": jnp.pad(c_w_in[0], ((0, 0), (0, pad))).astype(BF16),
        "c_wk_t": c_w_in[0][:, W:2 * W].T.astype(BF16),
        "c_gate_b": jnp.pad(c_gate_b[0], (0, pad)).reshape(1, GATE_PAD),
        "c_head_norm": c_head_norm[0].reshape(1, W),
        "c_w_out": c_w_out[0].astype(BF16),
        "mlp_w1_0": mlp_w1[0].astype(BF16),
        "mlp_w2_0": mlp_w2[0].astype(BF16),
        "mlp_w1_1": mlp_w1[1].astype(BF16),
        "mlp_w2_1": mlp_w2[1].astype(BF16),
        "norm_final": norm_final.reshape(1, D),
    }
    y_prompt = _trunk(x_prompt, mod, 0, p)
    y_sample = _trunk(x_sample, mod, n_p, p)
    return (y_prompt, y_sample)
```

```python
import functools

import jax
import jax.numpy as jnp
from jax import lax
from jax.experimental import pallas as pl
from jax.experimental.pallas import tpu as pltpu

D_MODEL = 1024
DEPTH = 2
CONV_CH = 512
CONV_WIDTH = 31
CONV_HALO = 16
ATT_HEADS = 8
ATT_HEAD_DIM = 64
ATT_WIDTH = ATT_HEADS * ATT_HEAD_DIM
DILATED_PATTERNS = ((128, 1), (512, 4), (2048, 16))
N_PATTERNS = len(DILATED_PATTERNS)
ATT_RADIUS = 64
ROPE_THETA = 500000.0
ROPE_DIM = ATT_HEAD_DIM // 4
ROPE_HALF = ROPE_DIM // 2
AB_IN = 2 * CONV_CH + N_PATTERNS * 3 * ATT_WIDTH
QKV_W = N_PATTERNS * 3 * ATT_WIDTH
MLSTM_HEADS = 8
MLSTM_WIDTH = D_MODEL
MLSTM_HEAD_DIM = MLSTM_WIDTH // MLSTM_HEADS
MLSTM_CHUNK = 128
N_GATES = 4 * MLSTM_HEADS
GATE_PAD = 128
D_FF = 4 * D_MODEL
EPS = 1e-6
NEG_INF = -1e30
LOG2_E = 1.4426950408889634
LN_2 = 0.6931471805599453

LANES = 128
VMEM_LIMIT = 56 * 1024 * 1024
ROW_TILE = 512
ATT_TQ = 128
ATT_TILES_PER_STEP = 8
CONV_ROWS = 32
CONV_CHAINS = 4
NORM_ROWS = 128
MLP_FF_TILE = 1024
MLSTM_SEQS_PER_STEP = 4

F32 = jnp.float32
BF16 = jnp.bfloat16


def _params(*sem):
    return pltpu.CompilerParams(dimension_semantics=sem, vmem_limit_bytes=VMEM_LIMIT)


def _norm_mod(x, gain, scale, shift):
    y = x * lax.rsqrt(jnp.mean(x * x, axis=-1, keepdims=True) + EPS)
    return (y * gain) * (1.0 + scale) + shift


def _mod_kernel(c_ref, w_ref, b_ref, o_ref):
    c = c_ref[...]
    s = c * jax.nn.sigmoid(c)
    o_ref[...] = jnp.dot(s, w_ref[...], preferred_element_type=F32,
                         precision=lax.Precision.HIGHEST) + b_ref[...]


def _modulation(c_all, ada_w, ada_b):
    R = c_all.shape[0]
    D = D_MODEL
    out = pl.pallas_call(
        _mod_kernel,
        grid=(DEPTH, 6),
        in_specs=[
            pl.BlockSpec((R, D), lambda l, j: (0, 0)),
            pl.BlockSpec((None, D, D), lambda l, j: (l, 0, j)),
            pl.BlockSpec((None, 1, D), lambda l, j: (l, 0, j)),
        ],
        out_specs=pl.BlockSpec((None, R, D), lambda l, j: (l, 0, j)),
        out_shape=jax.ShapeDtypeStruct((DEPTH, R, 6 * D), F32),
        compiler_params=_params("arbitrary", "arbitrary"),
        name="adaln_mod",
    )(c_all, ada_w, ada_b.reshape(DEPTH, 1, 6 * D))
    return out.reshape(DEPTH, R, 6, D)


def _rope_tables(S):
    inv_freq = jnp.power(ROPE_THETA, -jnp.arange(ROPE_HALF, dtype=F32) / ROPE_HALF)
    ang = jnp.arange(S).astype(F32)[:, None] * inv_freq[None, :]
    cos, sin = jnp.cos(ang), jnp.sin(ang)
    ones = jnp.ones((S, ATT_HEAD_DIM - ROPE_DIM), F32)
    zeros = jnp.zeros((S, ATT_HEAD_DIM - ROPE_DIM), F32)
    zh = jnp.zeros((S, ROPE_HALF), F32)
    cos_t = jnp.concatenate([cos, cos, ones], axis=1)
    sin_up = jnp.concatenate([-sin, zh, zeros], axis=1)
    sin_dn = jnp.concatenate([zh, sin, zeros], axis=1)
    rep = LANES // ATT_HEAD_DIM
    return tuple(jnp.tile(t, (1, rep)) for t in (cos_t, sin_up, sin_dn))


def _ab_in_kernel(x_ref, mod_ref, gain_ref, w_ref, *rest, tm):
    tab_refs = rest[:3]
    a_ref = rest[3]
    out_refs = rest[4:4 + N_PATTERNS]
    h_ref = rest[-1]
    hf = _norm_mod(x_ref[...], gain_ref[...], mod_ref[1:2, :], mod_ref[0:1, :])
    n_slab = hf.shape[1] // LANES
    for c in range(n_slab):
        h_ref[c] = hf[:, c * LANES:(c + 1) * LANES]
    h = hf.astype(BF16)

    def seg(lhs, j):
        return jnp.dot(lhs, w_ref[:, j * ATT_WIDTH:(j + 1) * ATT_WIDTH], preferred_element_type=F32)

    a_ref[...] = (seg(h, 0) * jax.nn.sigmoid(seg(h, 1))).astype(a_ref.dtype)

    def rope(p, tabs, scale):
        cos_t, sin_up, sin_dn = tabs
        parts = []
        for j in range(ATT_WIDTH // LANES):
            t = p[:, j * LANES:(j + 1) * LANES]
            up = pltpu.roll(t, LANES - ROPE_HALF, 1)
            dn = pltpu.roll(t, ROPE_HALF, 1)
            r = t * cos_t + up * sin_up + dn * sin_dn
            if scale != 1.0:
                r = r * scale
            parts.append(r.astype(BF16))
        return jnp.concatenate(parts, axis=1)

    for g, (_, dil) in enumerate(DILATED_PATTERNS):
        lt = tm // dil
        if dil == 1:
            hg = h
            tabs = tuple(t[...] for t in tab_refs)
        else:
            hg = jnp.concatenate(
                [jnp.concatenate([h_ref[c, pl.ds(r, lt, stride=dil), :] for c in range(n_slab)], axis=1)
                 for r in range(dil)], axis=0).astype(BF16)
            tabs = tuple(jnp.concatenate([t[pl.ds(r, lt, stride=dil), :] for r in range(dil)], axis=0)
                         for t in tab_refs)
        q = rope(seg(hg, 2 + 3 * g), tabs, ATT_HEAD_DIM ** -0.5 * LOG2_E)
        k = rope(seg(hg, 3 + 3 * g), tabs, 1.0)
        v = seg(hg, 4 + 3 * g).astype(BF16)
        for c, val in enumerate((q, k, v)):
            for r in range(dil):
                out_refs[g][r, :, c * ATT_WIDTH:(c + 1) * ATT_WIDTH] = val[r * lt:(r + 1) * lt, :]


def _ab_in_proj(x, mod, n_off, gain, w_in, tables):
    N, S, D = x.shape
    tm = min(ROW_TILE, S)
    tab_spec = pl.BlockSpec((tm, LANES), lambda n, i: (i, 0))
    out_specs = [pl.BlockSpec((None, tm, CONV_CH), lambda n, i: (n, i, 0))]
    out_shape = [jax.ShapeDtypeStruct((N, S, CONV_CH), BF16)]
    for _, dil in DILATED_PATTERNS:
        out_specs.append(pl.BlockSpec((None, dil, tm // dil, 3 * ATT_WIDTH), lambda n, i: (n, 0, i, 0)))
        out_shape.append(jax.ShapeDtypeStruct((N, dil, S // dil, 3 * ATT_WIDTH), BF16))
    return pl.pallas_call(
        functools.partial(_ab_in_kernel, tm=tm),
        grid=(N, S // tm),
        in_specs=[
            pl.BlockSpec((None, tm, D), lambda n, i: (n, i, 0)),
            pl.BlockSpec((None, None, 6, D), lambda n, i: (0, n + n_off, 0, 0)),
            pl.BlockSpec((1, D), lambda n, i: (0, 0)),
            pl.BlockSpec((D, AB_IN), lambda n, i: (0, 0)),
        ] + [tab_spec] * 3,
        out_specs=out_specs,
        out_shape=out_shape,
        scratch_shapes=[pltpu.VMEM((D // LANES, tm, LANES), F32)],
        compiler_params=_params("parallel", "parallel"),
        name="ab_in_proj",
    )(x, mod, gain, w_in, *tables)


def _attn_kernel(q_ref, kp_ref, kc_ref, kn_ref, vp_ref, vc_ref, vn_ref, o_ref, l_ref, *, tq, n_sub, n_seq, n_blocks):
    i = pl.program_id(2)
    R = ATT_RADIUS
    bq = n_sub * tq
    tk = tq + 2 * R
    row = lax.broadcasted_iota(jnp.int32, (tq, tk), 0)
    col = lax.broadcasted_iota(jnp.int32, (tq, tk), 1)
    rel = col - row
    band = (rel >= 0) & (rel <= 2 * R)
    masks = []
    for t in range(n_sub):
        mask = band
        if t == 0:
            mask = mask & (col >= jnp.where(i == 0, R, 0))
        if t == n_sub - 1:
            mask = mask & (col < jnp.where(i == n_blocks - 1, tq + R, tk))
        masks.append(mask)
    lane = lax.broadcasted_iota(jnp.int32, (1, LANES), 1)
    first_head = lane < ATT_HEAD_DIM

    def window(p_ref, c_ref, n_ref, r, t, sl):
        lo, hi = t * tq - R, (t + 1) * tq + R
        parts = []
        if lo < 0:
            parts.append(p_ref[r, :, sl])
        parts.append(c_ref[r, max(lo, 0):min(hi, bq), sl])
        if hi > bq:
            parts.append(n_ref[r, :, sl])
        return jnp.concatenate(parts, axis=0)

    for r in range(n_seq):
        for t in range(n_sub):
            rows = slice(t * tq, (t + 1) * tq)
            lse_all = jnp.zeros((tq, LANES), F32)
            for j in range(ATT_WIDTH // LANES):
                sl = slice(j * LANES, (j + 1) * LANES)
                q = q_ref[r, rows, sl]
                kw = window(kp_ref, kc_ref, kn_ref, r, t, sl)
                vw = window(vp_ref, vc_ref, vn_ref, r, t, sl)
                v1 = jnp.concatenate([vw, jnp.ones_like(vw)], axis=1)
                o_pair = None
                for hh in range(2):
                    sel = first_head if hh == 0 else jnp.logical_not(first_head)
                    qm = jnp.where(sel, q, jnp.zeros_like(q))
                    s = lax.dot_general(qm, kw, (((1,), (1,)), ((), ())), preferred_element_type=F32)
                    s = jnp.where(masks[t], s, NEG_INF)
                    m = jnp.max(s, axis=-1, keepdims=True)
                    p = jnp.exp2((s - m).astype(BF16))
                    ol = jnp.dot(p, v1, preferred_element_type=F32)
                    l = ol[:, LANES:]
                    o_h = ol[:, 0:LANES] / l
                    o_pair = o_h if hh == 0 else jnp.where(first_head, o_pair, o_h)
                    lse_all = jnp.where(lane == 2 * j + hh, (m + jnp.log2(l)) * LN_2, lse_all)
                o_ref[r, rows, sl] = o_pair.astype(o_ref.dtype)
            l_ref[r, rows, :] = lse_all


def _dilated_attention(qkv_g, g):
    N, dil, L, _ = qkv_g.shape
    tq = min(ATT_TQ, L)
    n_sub = min(ATT_TILES_PER_STEP, L // tq)
    n_seq = min(ATT_TILES_PER_STEP // n_sub, dil)
    bq = n_sub * tq
    nb = L // bq
    hb = bq // ATT_RADIUS
    n_halo = L // ATT_RADIUS

    def cur(c):
        return pl.BlockSpec((None, n_seq, bq, ATT_WIDTH), lambda n, r, i: (n, r, i, c))

    def before(c):
        return pl.BlockSpec((None, n_seq, ATT_RADIUS, ATT_WIDTH),
                            lambda n, r, i: (n, r, jnp.maximum(i * hb - 1, 0), c))

    def after(c):
        return pl.BlockSpec((None, n_seq, ATT_RADIUS, ATT_WIDTH),
                            lambda n, r, i: (n, r, jnp.minimum((i + 1) * hb, n_halo - 1), c))

    return pl.pallas_call(
        functools.partial(_attn_kernel, tq=tq, n_sub=n_sub, n_seq=n_seq, n_blocks=nb),
        grid=(N, dil // n_seq, nb),
        in_specs=[cur(0), before(1), cur(1), after(1), before(2), cur(2), after(2)],
        out_specs=[
            pl.BlockSpec((None, n_seq, bq, ATT_WIDTH), lambda n, r, i: (n, r, i, 0)),
            pl.BlockSpec((None, n_seq, bq, LANES), lambda n, r, i: (n, r, i, 0)),
        ],
        out_shape=[
            jax.ShapeDtypeStruct((N, dil, L, ATT_WIDTH), BF16),
            jax.ShapeDtypeStruct((N, dil, L, LANES), F32),
        ],
        compiler_params=_params("parallel", "parallel", "parallel"),
        name=f"dilated_attn_{g}",
    )(*([qkv_g] * 7))


def _squared_relu_mlp(h, w1_ref, w2_ref):
    acc = None
    for c in range(D_FF // MLP_FF_TILE):
        sl = slice(c * MLP_FF_TILE, (c + 1) * MLP_FF_TILE)
        u = jnp.maximum(jnp.dot(h, w1_ref[:, sl], preferred_element_type=F32), 0.0)
        part = jnp.dot((u * u).astype(BF16), w2_ref[sl, :], preferred_element_type=F32)
        acc = part if acc is None else acc + part
    return acc


def _ab_tail_kernel(ap_ref, ac_ref, an_ref, o0_ref, l0_ref, o1_ref, l1_ref, o2_ref, l2_ref, x_ref, mod_ref,
                    cw_ref, cb_ref, lg_ref, lb_ref, wo_ref, gain_ref, w1_ref, w2_ref, o_ref,
                    sh_ref, conv_ref, on_ref, ln_ref, act_ref, *, tm, n_tiles):
    i = pl.program_id(1)
    H = CONV_HALO
    SUB = 8

    sh_ref[0, 0:H, :] = jnp.where(i > 0, ap_ref[...].astype(F32), 0.0)
    sh_ref[0, H:H + tm, :] = ac_ref[...].astype(F32)
    sh_ref[0, H + tm:H + tm + H, :] = jnp.where(i < n_tiles - 1, an_ref[...].astype(F32), 0.0)
    span = tm + 2 * H - SUB
    for s in range(1, SUB):
        sh_ref[s, 0:span, :] = sh_ref[0, s:s + span, :]

    off = H - CONV_WIDTH // 2
    for c in range(CONV_CH // LANES):
        lanes = slice(c * LANES, (c + 1) * LANES)
        taps = [jnp.broadcast_to(cw_ref[j:j + 1, lanes], (CONV_ROWS, LANES)) for j in range(CONV_WIDTH)]
        bias = jnp.broadcast_to(cb_ref[:, lanes], (CONV_ROWS, LANES))
        for k in range(tm // CONV_ROWS):
            r0 = k * CONV_ROWS
            accs = [bias] + [None] * (CONV_CHAINS - 1)
            for j in range(CONV_WIDTH):
                s, blk = (off + j) % SUB, (off + j) // SUB
                term = taps[j] * sh_ref[s, pl.ds(r0 + SUB * blk, CONV_ROWS), lanes]
                a = accs[j % CONV_CHAINS]
                accs[j % CONV_CHAINS] = term if a is None else a + term
            while len(accs) > 1:
                accs = [accs[u] + accs[u + 1] for u in range(0, len(accs), 2)]
            conv_ref[pl.ds(r0, CONV_ROWS), lanes] = accs[0]

    def norm_chunk(k, carry):
        r0 = pl.multiple_of(k * NORM_ROWS, NORM_ROWS)
        acc = conv_ref[pl.ds(r0, NORM_ROWS), :]
        mu = jnp.mean(acc, axis=-1, keepdims=True)
        xc = acc - mu
        y = xc * lax.rsqrt(jnp.mean(xc * xc, axis=-1, keepdims=True) + EPS)
        y = y * lg_ref[...] + lb_ref[...]
        act_ref[pl.ds(r0, NORM_ROWS), :] = (y * jax.nn.sigmoid(y)).astype(act_ref.dtype)
        return carry

    lax.fori_loop(0, tm // NORM_ROWS, norm_chunk, 0)

    for g, (o_g, l_g) in enumerate(((o0_ref, l0_ref), (o1_ref, l1_ref), (o2_ref, l2_ref))):
        dil = DILATED_PATTERNS[g][1]
        lt = tm // dil
        for r in range(dil):
            rows = pl.ds(r, lt, stride=dil) if dil > 1 else pl.ds(0, tm)
            o_r = o_g[r].astype(F32)
            for c in range(ATT_WIDTH // LANES):
                on_ref[g, c, rows, :] = o_r[:, c * LANES:(c + 1) * LANES]
            ln_ref[g, rows, :] = l_g[r]
    lses = [ln_ref[g] for g in range(N_PATTERNS)]
    mx = functools.reduce(jnp.maximum, lses)
    es = [jnp.exp(l - mx) for l in lses]
    tot = functools.reduce(lambda a, b: a + b, es)
    erow = lax.broadcasted_iota(jnp.int32, (LANES, ATT_WIDTH), 0)
    ecol = lax.broadcasted_iota(jnp.int32, (LANES, ATT_WIDTH), 1)
    expand = (ecol // ATT_HEAD_DIM == erow).astype(BF16)
    att = None
    for g in range(N_PATTERNS):
        wgt = es[g] / tot
        hi = wgt.astype(BF16)
        lo = (wgt - hi.astype(F32)).astype(BF16)
        wide = jnp.dot(hi, expand, preferred_element_type=F32) + jnp.dot(lo, expand, preferred_element_type=F32)
        term = wide * jnp.concatenate([on_ref[g, c] for c in range(ATT_WIDTH // LANES)], axis=1)
        att = term if att is None else att + term

    mix = jnp.dot(act_ref[...], wo_ref[0:CONV_CH, :], preferred_element_type=F32)
    mix = mix + jnp.dot(att.astype(BF16), wo_ref[CONV_CH:, :], preferred_element_type=F32)
    x1 = x_ref[...] + mod_ref[2:3, :] * mix
    h2 = _norm_mod(x1, gain_ref[...], mod_ref[4:5, :], mod_ref[3:4, :]).astype(BF16)
    o_ref[...] = x1 + mod_ref[5:6, :] * _squared_relu_mlp(h2, w1_ref, w2_ref)


def _ab_tail(a, att_parts, x, mod, n_off, conv_w, conv_b, ln_g, ln_b, w_out, gain, w1, w2):
    N, S, D = x.shape
    tm = min(ROW_TILE, S)
    nt = S // tm
    hb = tm // CONV_HALO
    n_halo = S // CONV_HALO
    row = lambda n, i: (n, i, 0)
    const = lambda n, i: (0, 0)
    once = pl.Buffered(1)
    att_specs, att_args = [], []
    for (_, dil), (o_g, l_g) in zip(DILATED_PATTERNS, att_parts):
        att_specs.append(pl.BlockSpec((None, dil, tm // dil, ATT_WIDTH), lambda n, i: (n, 0, i, 0)))
        att_specs.append(pl.BlockSpec((None, dil, tm // dil, LANES), lambda n, i: (n, 0, i, 0)))
        att_args += [o_g, l_g]
    return pl.pallas_call(
        functools.partial(_ab_tail_kernel, tm=tm, n_tiles=nt),
        grid=(N, nt),
        in_specs=[
            pl.BlockSpec((None, CONV_HALO, CONV_CH), lambda n, i: (n, jnp.maximum(i * hb - 1, 0), 0)),
            pl.BlockSpec((None, tm, CONV_CH), row),
            pl.BlockSpec((None, CONV_HALO, CONV_CH), lambda n, i: (n, jnp.minimum((i + 1) * hb, n_halo - 1), 0)),
        ] + att_specs + [
            pl.BlockSpec((None, tm, D), row),
            pl.BlockSpec((None, None, 6, D), lambda n, i: (0, n + n_off, 0, 0)),
            pl.BlockSpec((CONV_WIDTH, CONV_CH), const),
            pl.BlockSpec((1, CONV_CH), const),
            pl.BlockSpec((1, CONV_CH), const),
            pl.BlockSpec((1, CONV_CH), const),
            pl.BlockSpec((D, D), const, pipeline_mode=once),
            pl.BlockSpec((1, D), const),
            pl.BlockSpec((D, D_FF), const, pipeline_mode=once),
            pl.BlockSpec((D_FF, D), const, pipeline_mode=once),
        ],
        out_specs=pl.BlockSpec((None, tm, D), row),
        out_shape=jax.ShapeDtypeStruct((N, S, D), F32),
        scratch_shapes=[
            pltpu.VMEM((8, tm + 2 * CONV_HALO, CONV_CH), F32),
            pltpu.VMEM((tm, CONV_CH), F32),
            pltpu.VMEM((N_PATTERNS, ATT_WIDTH // LANES, tm, LANES), F32),
            pltpu.VMEM((N_PATTERNS, tm, LANES), F32),
            pltpu.VMEM((tm, CONV_CH), BF16),
        ],
        compiler_params=_params("parallel", "parallel"),
        name="ab_tail",
    )(a, a, a, *att_args, x, mod, conv_w, conv_b, ln_g, ln_b, w_out, gain, w1, w2)


def _c_tail_kernel(hf_ref, hb_ref, og_ref, x_ref, mod_ref, hn_ref, wo_ref, gain_ref, w1_ref, w2_ref, fg_ref, o_ref):
    dh = MLSTM_HEAD_DIM
    parts = []
    for hd in range(MLSTM_HEADS):
        sl = slice(hd * dh, (hd + 1) * dh)
        ht = hf_ref[:, sl].astype(F32) + hb_ref[:, sl].astype(F32)
        ht = ht * lax.rsqrt(jnp.mean(ht * ht, axis=-1, keepdims=True) + EPS)
        parts.append((og_ref[:, sl].astype(F32) * (ht * hn_ref[:, sl])).astype(BF16))
    z = jnp.concatenate(parts, axis=1)
    mix = jnp.dot(z, wo_ref[...], preferred_element_type=F32)
    x1 = x_ref[...] + mod_ref[2:3, :] * mix
    h2 = _norm_mod(x1, gain_ref[...], mod_ref[4:5, :], mod_ref[3:4, :]).astype(BF16)
    y = x1 + mod_ref[5:6, :] * _squared_relu_mlp(h2, w1_ref, w2_ref)
    o_ref[...] = y * lax.rsqrt(jnp.mean(y * y, axis=-1, keepdims=True) + EPS) * fg_ref[...]


def _c_tail(hf, hb, og, x, mod, n_off, head_norm, w_out, gain, w1, w2, final_gain):
    N, S, D = x.shape
    W = MLSTM_WIDTH
    tm = min(ROW_TILE, S)
    row = lambda n, i: (n, i, 0)
    const = lambda n, i: (0, 0)
    once = pl.Buffered(1)
    return pl.pallas_call(
        _c_tail_kernel,
        grid=(N, S // tm),
        in_specs=[
            pl.BlockSpec((None, tm, W), row),
            pl.BlockSpec((None, tm, W), row),
            pl.BlockSpec((None, tm, W), row),
            pl.BlockSpec((None, tm, D), row),
            pl.BlockSpec((None, None, 6, D), lambda n, i: (1, n + n_off, 0, 0)),
            pl.BlockSpec((1, W), const),
            pl.BlockSpec((W, D), const, pipeline_mode=once),
            pl.BlockSpec((1, D), const),
            pl.BlockSpec((D, D_FF), const, pipeline_mode=once),
            pl.BlockSpec((D_FF, D), const, pipeline_mode=once),
            pl.BlockSpec((1, D), const),
        ],
        out_specs=pl.BlockSpec((None, tm, D), row),
        out_shape=jax.ShapeDtypeStruct((N, S, D), F32),
        compiler_params=_params("parallel", "parallel"),
        name="c_tail",
    )(hf, hb, og, x, mod, head_norm, w_out, gain, w1, w2, final_gain)


def _c_in_kernel(x_ref, mod_ref, gain_ref, w_ref, wkt_ref, gb_ref, q_ref, kt_ref, v_ref, og_ref, gate_ref):
    W = MLSTM_WIDTH
    x = x_ref[...]
    h = _norm_mod(x, gain_ref[...], mod_ref[1:2, :], mod_ref[0:1, :]).astype(BF16)

    def seg(lo, width):
        return jnp.dot(h, w_ref[:, lo:lo + width], preferred_element_type=F32)

    q_ref[...] = seg(0, W).astype(q_ref.dtype)
    kt = lax.dot_general(wkt_ref[...], h, (((1,), (1,)), ((), ())), preferred_element_type=F32)
    kt_ref[...] = (kt * (MLSTM_HEAD_DIM ** -0.5)).astype(kt_ref.dtype)
    v_ref[...] = seg(2 * W, W).astype(v_ref.dtype)
    og_ref[...] = jax.nn.sigmoid(seg(3 * W, W)).astype(og_ref.dtype)
    z = seg(4 * W, GATE_PAD) + gb_ref[...]
    lane = lax.broadcasted_iota(jnp.int32, (1, GATE_PAD), 1)
    is_forget = (lane % (2 * MLSTM_HEADS)) >= MLSTM_HEADS
    log_sig = jnp.minimum(z, 0.0) - jnp.log1p(jnp.exp(-jnp.abs(z)))
    gate_ref[...] = jnp.where(is_forget, log_sig, z)


def _c_in_proj(x, mod, n_off, gain, w_in, wk_t, gate_b):
    N, S, D = x.shape
    W = MLSTM_WIDTH
    tm = min(ROW_TILE, S)
    row = lambda n, i: (n, i, 0)
    const = lambda n, i: (0, 0)
    return pl.pallas_call(
        _c_in_kernel,
        grid=(N, S // tm),
        in_specs=[
            pl.BlockSpec((None, tm, D), row),
            pl.BlockSpec((None, None, 6, D), lambda n, i: (1, n + n_off, 0, 0)),
            pl.BlockSpec((1, D), const),
            pl.BlockSpec((D, 4 * W + GATE_PAD), const),
            pl.BlockSpec((W, D), const),
            pl.BlockSpec((1, GATE_PAD), const),
        ],
        out_specs=[
            pl.BlockSpec((None, tm, W), row),
            pl.BlockSpec((None, W, tm), lambda n, i: (n, 0, i)),
            pl.BlockSpec((None, tm, W), row),
            pl.BlockSpec((None, tm, W), row),
            pl.BlockSpec((None, tm, GATE_PAD), row),
        ],
        out_shape=[
            jax.ShapeDtypeStruct((N, S, W), BF16),
            jax.ShapeDtypeStruct((N, W, S), BF16),
            jax.ShapeDtypeStruct((N, S, W), BF16),
            jax.ShapeDtypeStruct((N, S, W), BF16),
            jax.ShapeDtypeStruct((N, S, GATE_PAD), F32),
        ],
        compiler_params=_params("parallel", "parallel"),
        name="c_in_proj",
    )(x, mod, gain, w_in, wk_t, gate_b)


def _split3(x):
    hi = x.astype(BF16)
    r1 = x - hi.astype(F32)
    mid = r1.astype(BF16)
    lo = (r1 - mid.astype(F32)).astype(BF16)
    return hi, mid, lo


def _mlstm_kernel(qf_ref, kf_ref, vf_ref, gf_ref, qb_ref, kb_ref, vb_ref, gb_ref,
                  hf_ref, hb_ref, st_ref, m_ref, ml_ref, *, n_seq):
    L = MLSTM_CHUNK
    dh = MLSTM_HEAD_DIM
    H = MLSTM_HEADS

    @pl.when(pl.program_id(1) == 0)
    def _():
        st_ref[...] = jnp.zeros_like(st_ref)
        m_ref[...] = jnp.zeros_like(m_ref)
        ml_ref[...] = jnp.zeros_like(ml_ref)

    row = lax.broadcasted_iota(jnp.int32, (L, L), 0)
    col = lax.broadcasted_iota(jnp.int32, (L, L), 1)
    ones_blk = jnp.ones((L, dh), BF16)

    dirs = (
        (qf_ref, kf_ref, vf_ref, gf_ref, hf_ref, row >= col, L - 1),
        (qb_ref, kb_ref, vb_ref, gb_ref, hb_ref, row <= col, 0),
    )
    streams = [(b, d) for b in range(n_seq) for d in range(2)]
    pre = []
    for s, (b, d) in enumerate(streams):
        q_ref, kt_ref, v_ref, g_ref, h_ref, allowed, last = dirs[d]
        gates = g_ref[b]
        tri = allowed.astype(BF16)
        csum = sum(jnp.dot(tri, piece, preferred_element_type=F32) for piece in _split3(gates))
        gates_t = gates.T
        csum_t = csum.T
        lo = 2 * H * d
        ig_t = gates_t[lo:lo + H, :]
        b_t = csum_t[lo + H:lo + 2 * H, :]
        crow = ig_t - b_t
        b_last = b_t[:, last:last + 1]
        m_old = m_ref[s]
        wlog = b_last + crow
        m_new = jnp.maximum(b_last + m_old, jnp.max(wlog, axis=-1, keepdims=True))
        decay = jnp.exp(b_last + m_old - m_new)
        wrow = jnp.exp(wlog - m_new)
        m_ref[s] = m_new

        c_col = pltpu.roll(gates, H, 1) - csum
        run = c_col
        k = 1
        while k < L:
            pad = jnp.full((k, GATE_PAD), NEG_INF, F32)
            moved = (jnp.concatenate([pad, run[:L - k]], axis=0) if d == 0
                     else jnp.concatenate([run[k:], pad], axis=0))
            run = jnp.maximum(run, moved)
            k *= 2
        m_lane = ml_ref[s]
        inter_c = csum + m_lane
        mt_c = jnp.maximum(inter_c, csum + run)
        b_last_c = csum[last:last + 1, :]
        ml_ref[s] = jnp.maximum(b_last_c + m_lane, jnp.max(b_last_c + c_col, axis=0, keepdims=True))
        pre.append((csum - mt_c, jnp.exp(inter_c - mt_c), jnp.exp(-mt_c), crow, decay, wrow))

    units = [(s, hd) for hd in range(H) for s in range(len(streams))]

    stage_a = []
    for s, hd in units:
        b, d = streams[s]
        q_ref, kt_ref, v_ref, g_ref, h_ref, allowed, last = dirs[d]
        a_c, ei_c, emt_c, crow, decay, wrow = pre[s]
        sl = slice(hd * dh, (hd + 1) * dh)
        cf = 2 * H * d + H + hd
        q = q_ref[b, :, sl]
        kt = kt_ref[b, sl, :]
        column = lambda arr: jnp.broadcast_to(arr[:, cf:cf + 1], (L, L))
        e_intra = jnp.exp(jnp.where(allowed, column(a_c) + crow[hd:hd + 1, :], NEG_INF))
        sc = jnp.dot(q, kt, preferred_element_type=F32) * e_intra
        lhs = jnp.concatenate([sc.astype(BF16), (column(ei_c) * q.astype(F32)).astype(BF16)], axis=1)
        stage_a.append((lhs, column(emt_c)))

    for (s, hd), (lhs, emt) in zip(units, stage_a):
        b, d = streams[s]
        q_ref, kt_ref, v_ref, g_ref, h_ref, allowed, last = dirs[d]
        sl = slice(hd * dh, (hd + 1) * dh)
        v1 = jnp.concatenate([v_ref[b, :, sl], ones_blk], axis=1)
        rhs = jnp.concatenate([v1, st_ref[s, hd].astype(BF16)], axis=0)
        nd = jnp.dot(lhs, rhs, preferred_element_type=F32)
        den = jnp.maximum(jnp.abs(nd[:, dh:]), emt)
        h_ref[b, :, sl] = (nd[:, 0:dh] / den).astype(h_ref.dtype)

    for s, hd in units:
        b, d = streams[s]
        q_ref, kt_ref, v_ref, g_ref, h_ref, allowed, last = dirs[d]
        a_c, ei_c, emt_c, crow, decay, wrow = pre[s]
        sl = slice(hd * dh, (hd + 1) * dh)
        v1 = jnp.concatenate([v_ref[b, :, sl], ones_blk], axis=1)
        kw = (kt_ref[b, sl, :].astype(F32) * wrow[hd:hd + 1, :]).astype(BF16)
        dec = decay[hd:hd + 1, :]
        upd = jnp.dot(kw, v1, preferred_element_type=F32)
        st_ref[s, hd] = jnp.concatenate([dec, dec], axis=1) * st_ref[s, hd] + upd


def _mlstm_scan(q, kt, v, gates):
    N, S, W = q.shape
    L = MLSTM_CHUNK
    nc = S // L
    nb = min(MLSTM_SEQS_PER_STEP, N)
    rows_f = lambda n, i: (n, i, 0)
    rows_b = lambda n, i: (n, nc - 1 - i, 0)
    cols_f = lambda n, i: (n, 0, i)
    cols_b = lambda n, i: (n, 0, nc - 1 - i)
    blk = lambda im: pl.BlockSpec((nb, L, W), im)
    tblk = lambda im: pl.BlockSpec((nb, W, L), im)
    gblk = lambda im: pl.BlockSpec((nb, L, GATE_PAD), im)
    return pl.pallas_call(
        functools.partial(_mlstm_kernel, n_seq=nb),
        grid=(N // nb, nc),
        in_specs=[blk(rows_f), tblk(cols_f), blk(rows_f), gblk(rows_f),
                  blk(rows_b), tblk(cols_b), blk(rows_b), gblk(rows_b)],
        out_specs=[blk(rows_f), blk(rows_b)],
        out_shape=[jax.ShapeDtypeStruct((N, S, W), BF16)] * 2,
        scratch_shapes=[
            pltpu.VMEM((2 * nb, MLSTM_HEADS, MLSTM_HEAD_DIM, 2 * MLSTM_HEAD_DIM), F32),
            pltpu.VMEM((2 * nb, MLSTM_HEADS, LANES), F32),
            pltpu.VMEM((2 * nb, 1, GATE_PAD), F32),
        ],
        compiler_params=_params("parallel", "arbitrary"),
        name="mlstm_scan",
    )(q, kt, v, gates, q, kt, v, gates)


def _trunk(x, mod, n_off, p):
    N, S, D = x.shape
    tables = _rope_tables(S)
    a, *qkv = _ab_in_proj(x, mod, n_off, p["norm_mix0"], p["ab_w_in"], tables)
    att_parts = [_dilated_attention(qkv[g], g) for g in range(N_PATTERNS)]
    x = _ab_tail(a, att_parts, x, mod, n_off, p["conv_w"], p["conv_b"], p["conv_ln_g"], p["conv_ln_b"],
                 p["ab_w_out"], p["norm_mlp0"], p["mlp_w1_0"], p["mlp_w2_0"])
    q1, kt1, v1, og, gates = _c_in_proj(x, mod, n_off, p["norm_mix1"], p["c_w_in"], p["c_wk_t"], p["c_gate_b"])
    hf, hb = _mlstm_scan(q1, kt1, v1, gates)
    return _c_tail(hf, hb, og, x, mod, n_off, p["c_head_norm"], p["c_w_out"], p["norm_mlp1"],
                   p["mlp_w1_1"], p["mlp_w2_1"], p["norm_final"])


def kernel(x_prompt, x_sample, c_prompt, c_sample, ada_w, ada_b, norm_mix, norm_mlp, ab_w_in, conv_w, conv_b, conv_ln_g, conv_ln_b, ab_w_out, c_w_in, c_gate_b, c_head_norm, c_w_out, mlp_w1, mlp_w2, norm_final):
    D = D_MODEL
    n_p, n_s = c_prompt.shape[0], c_sample.shape[0]
    rows = n_p + n_s
    rows_pad = -(-rows // 8) * 8
    c_all = jnp.concatenate([c_prompt, c_sample, jnp.zeros((rows_pad - rows, D), F32)], axis=0)
    mod = _modulation(c_all, ada_w, ada_b)

    W = MLSTM_WIDTH
    pad = GATE_PAD - N_GATES
    p = {
        "norm_mix0": norm_mix[0].reshape(1, D),
        "norm_mix1": norm_mix[1].reshape(1, D),
        "norm_mlp0": norm_mlp[0].reshape(1, D),
        "norm_mlp1": norm_mlp[1].reshape(1, D),
        "ab_w_in": ab_w_in[0].astype(BF16),
        "conv_w": conv_w[0],
        "conv_b": conv_b[0].reshape(1, CONV_CH),
        "conv_ln_g": conv_ln_g[0].reshape(1, CONV_CH),
        "conv_ln_b": conv_ln_b[0].reshape(1, CONV_CH),
        "ab_w_out": ab_w_out[0].astype(BF16),
        "c_w_in": jnp.pad(c_w_in[0], ((0, 0), (0, pad))).astype(BF16),
        "c_wk_t": c_w_in[0][:, W:2 * W].T.astype(BF16),
        "c_gate_b": jnp.pad(c_gate_b[0], (0, pad)).reshape(1, GATE_PAD),
        "c_head_norm": c_head_norm[0].reshape(1, W),
        "c_w_out": c_w_out[0].astype(BF16),
        "mlp_w1_0": mlp_w1[0].astype(BF16),
        "mlp_w2_0": mlp_w2[0].astype(BF16),
        "mlp_w1_1": mlp_w1[1].astype(BF16),
        "mlp_w2_1": mlp_w2[1].astype(BF16),
        "norm_final": norm_final.reshape(1, D),
    }
    y_prompt = _trunk(x_prompt, mod, 0, p)
    y_sample = _trunk(x_sample, mod, n_p, p)
    return (y_prompt, y_sample)
```

```python
import functools

import jax
import jax.numpy as jnp
from jax import lax
from jax.experimental import pallas as pl
from jax.experimental.pallas import tpu as pltpu

D_MODEL = 1024
DEPTH = 2
CONV_CH = 512
CONV_WIDTH = 31
CONV_HALO = 16
ATT_HEADS = 8
ATT_HEAD_DIM = 64
ATT_WIDTH = ATT_HEADS * ATT_HEAD_DIM
DILATED_PATTERNS = ((128, 1), (512, 4), (2048, 16))
N_PATTERNS = len(DILATED_PATTERNS)
ATT_RADIUS = 64
ROPE_THETA = 500000.0
ROPE_DIM = ATT_HEAD_DIM // 4
ROPE_HALF = ROPE_DIM // 2
AB_IN = 2 * CONV_CH + N_PATTERNS * 3 * ATT_WIDTH
QKV_W = N_PATTERNS * 3 * ATT_WIDTH
MLSTM_HEADS = 8
MLSTM_WIDTH = D_MODEL
MLSTM_HEAD_DIM = MLSTM_WIDTH // MLSTM_HEADS
MLSTM_CHUNK = 128
N_GATES = 4 * MLSTM_HEADS
GATE_PAD = 128
D_FF = 4 * D_MODEL
EPS = 1e-6
NEG_INF = -1e30
LOG2_E = 1.4426950408889634

LANES = 128
VMEM_LIMIT = 56 * 1024 * 1024
ROW_TILE = 512
ATT_TQ = 128
ATT_TILES_PER_STEP = 8
CONV_ROWS = 32
CONV_CHAINS = 4
NORM_ROWS = 128
MLP_FF_TILE = 1024
MLSTM_SEQS_PER_STEP = 4

F32 = jnp.float32
BF16 = jnp.bfloat16


def _params(*sem):
    return pltpu.CompilerParams(dimension_semantics=sem, vmem_limit_bytes=VMEM_LIMIT)


def _norm_mod(x, gain, scale, shift):
    y = x * lax.rsqrt(jnp.mean(x * x, axis=-1, keepdims=True) + EPS)
    return (y * gain) * (1.0 + scale) + shift


def _mod_kernel(c_ref, w_ref, b_ref, o_ref):
    c = c_ref[...]
    s = c * jax.nn.sigmoid(c)
    o_ref[...] = jnp.dot(s, w_ref[...], preferred_element_type=F32,
                         precision=lax.Precision.HIGHEST) + b_ref[...]


def _modulation(c_all, ada_w, ada_b):
    R = c_all.shape[0]
    D = D_MODEL
    out = pl.pallas_call(
        _mod_kernel,
        grid=(DEPTH, 6),
        in_specs=[
            pl.BlockSpec((R, D), lambda l, j: (0, 0)),
            pl.BlockSpec((None, D, D), lambda l, j: (l, 0, j)),
            pl.BlockSpec((None, 1, D), lambda l, j: (l, 0, j)),
        ],
        out_specs=pl.BlockSpec((None, R, D), lambda l, j: (l, 0, j)),
        out_shape=jax.ShapeDtypeStruct((DEPTH, R, 6 * D), F32),
        compiler_params=_params("arbitrary", "arbitrary"),
        name="adaln_mod",
    )(c_all, ada_w, ada_b.reshape(DEPTH, 1, 6 * D))
    return out.reshape(DEPTH, R, 6, D)


def _rope_tables(S):
    inv_freq = jnp.power(ROPE_THETA, -jnp.arange(ROPE_HALF, dtype=F32) / ROPE_HALF)
    ang = jnp.arange(S).astype(F32)[:, None] * inv_freq[None, :]
    cos, sin = jnp.cos(ang), jnp.sin(ang)
    ones = jnp.ones((S, ATT_HEAD_DIM - ROPE_DIM), F32)
    zeros = jnp.zeros((S, ATT_HEAD_DIM - ROPE_DIM), F32)
    zh = jnp.zeros((S, ROPE_HALF), F32)
    cos_t = jnp.concatenate([cos, cos, ones], axis=1)
    sin_up = jnp.concatenate([-sin, zh, zeros], axis=1)
    sin_dn = jnp.concatenate([zh, sin, zeros], axis=1)
    rep = LANES // ATT_HEAD_DIM
    return tuple(jnp.tile(t, (1, rep)) for t in (cos_t, sin_up, sin_dn))


def _ab_in_kernel(x_ref, mod_ref, gain_ref, w_ref, *rest, tm):
    tab_refs = rest[:3]
    a_ref = rest[3]
    out_refs = rest[4:4 + N_PATTERNS]
    h_ref = rest[-1]
    hf = _norm_mod(x_ref[...], gain_ref[...], mod_ref[1:2, :], mod_ref[0:1, :])
    n_slab = hf.shape[1] // LANES
    for c in range(n_slab):
        h_ref[c] = hf[:, c * LANES:(c + 1) * LANES]
    h = hf.astype(BF16)

    def seg(lhs, j):
        return jnp.dot(lhs, w_ref[:, j * ATT_WIDTH:(j + 1) * ATT_WIDTH], preferred_element_type=F32)

    a_ref[...] = (seg(h, 0) * jax.nn.sigmoid(seg(h, 1))).astype(a_ref.dtype)

    def rope(p, tabs, scale):
        cos_t, sin_up, sin_dn = tabs
        parts = []
        for j in range(ATT_WIDTH // LANES):
            t = p[:, j * LANES:(j + 1) * LANES]
            up = pltpu.roll(t, LANES - ROPE_HALF, 1)
            dn = pltpu.roll(t, ROPE_HALF, 1)
            r = t * cos_t + up * sin_up + dn * sin_dn
            if scale != 1.0:
                r = r * scale
            parts.append(r.astype(BF16))
        return jnp.concatenate(parts, axis=1)

    for g, (_, dil) in enumerate(DILATED_PATTERNS):
        lt = tm // dil
        if dil == 1:
            hg = h
            tabs = tuple(t[...] for t in tab_refs)
        else:
            hg = jnp.concatenate(
                [jnp.concatenate([h_ref[c, pl.ds(r, lt, stride=dil), :] for c in range(n_slab)], axis=1)
                 for r in range(dil)], axis=0).astype(BF16)
            tabs = tuple(jnp.concatenate([t[pl.ds(r, lt, stride=dil), :] for r in range(dil)], axis=0)
                         for t in tab_refs)
        q = rope(seg(hg, 2 + 3 * g), tabs, ATT_HEAD_DIM ** -0.5 * LOG2_E)
        k = rope(seg(hg, 3 + 3 * g), tabs, 1.0)
        v = seg(hg, 4 + 3 * g).astype(BF16)
        for c, val in enumerate((q, k, v)):
            for r in range(dil):
                out_refs[g][r, :, c * ATT_WIDTH:(c + 1) * ATT_WIDTH] = val[r * lt:(r + 1) * lt, :]


def _ab_in_proj(x, mod, n_off, gain, w_in, tables):
    N, S, D = x.shape
    tm = min(ROW_TILE, S)
    tab_spec = pl.BlockSpec((tm, LANES), lambda n, i: (i, 0))
    out_specs = [pl.BlockSpec((None, tm, CONV_CH), lambda n, i: (n, i, 0))]
    out_shape = [jax.ShapeDtypeStruct((N, S, CONV_CH), BF16)]
    for _, dil in DILATED_PATTERNS:
        out_specs.append(pl.BlockSpec((None, dil, tm // dil, 3 * ATT_WIDTH), lambda n, i: (n, 0, i, 0)))
        out_shape.append(jax.ShapeDtypeStruct((N, dil, S // dil, 3 * ATT_WIDTH), BF16))
    return pl.pallas_call(
        functools.partial(_ab_in_kernel, tm=tm),
        grid=(N, S // tm),
        in_specs=[
            pl.BlockSpec((None, tm, D), lambda n, i: (n, i, 0)),
            pl.BlockSpec((None, None, 6, D), lambda n, i: (0, n + n_off, 0, 0)),
            pl.BlockSpec((1, D), lambda n, i: (0, 0)),
            pl.BlockSpec((D, AB_IN), lambda n, i: (0, 0)),
        ] + [tab_spec] * 3,
        out_specs=out_specs,
        out_shape=out_shape,
        scratch_shapes=[pltpu.VMEM((D // LANES, tm, LANES), F32)],
        compiler_params=_params("parallel", "parallel"),
        name="ab_in_proj",
    )(x, mod, gain, w_in, *tables)


def _attn_kernel(q_ref, kp_ref, kc_ref, kn_ref, vp_ref, vc_ref, vn_ref, o_ref, m_ref, l_ref,
                 *, tq, n_sub, n_seq, n_blocks):
    i = pl.program_id(2)
    R = ATT_RADIUS
    bq = n_sub * tq
    tk = tq + 2 * R
    row = lax.broadcasted_iota(jnp.int32, (tq, tk), 0)
    col = lax.broadcasted_iota(jnp.int32, (tq, tk), 1)
    rel = col - row
    band = (rel >= 0) & (rel <= 2 * R)
    masks = []
    for t in range(n_sub):
        mask = band
        if t == 0:
            mask = mask & (col >= jnp.where(i == 0, R, 0))
        if t == n_sub - 1:
            mask = mask & (col < jnp.where(i == n_blocks - 1, tq + R, tk))
        masks.append(mask)
    lane = lax.broadcasted_iota(jnp.int32, (1, LANES), 1)
    first_head = lane < ATT_HEAD_DIM

    def window(p_ref, c_ref, n_ref, r, t, sl):
        lo, hi = t * tq - R, (t + 1) * tq + R
        parts = []
        if lo < 0:
            parts.append(p_ref[r, :, sl])
        parts.append(c_ref[r, max(lo, 0):min(hi, bq), sl])
        if hi > bq:
            parts.append(n_ref[r, :, sl])
        return jnp.concatenate(parts, axis=0)

    for r in range(n_seq):
        for t in range(n_sub):
            rows = slice(t * tq, (t + 1) * tq)
            m_all = jnp.zeros((tq, LANES), F32)
            l_all = jnp.ones((tq, LANES), F32)
            for j in range(ATT_WIDTH // LANES):
                sl = slice(j * LANES, (j + 1) * LANES)
                q = q_ref[r, rows, sl]
                kw = window(kp_ref, kc_ref, kn_ref, r, t, sl)
                vw = window(vp_ref, vc_ref, vn_ref, r, t, sl)
                v1 = jnp.concatenate([vw, jnp.ones_like(vw)], axis=1)
                o_pair = None
                for hh in range(2):
                    sel = first_head if hh == 0 else jnp.logical_not(first_head)
                    qm = jnp.where(sel, q, jnp.zeros_like(q))
                    s = lax.dot_general(qm, kw, (((1,), (1,)), ((), ())), preferred_element_type=F32)
                    s = jnp.where(masks[t], s, NEG_INF)
                    m = jnp.max(s, axis=-1, keepdims=True)
                    p = jnp.exp2((s - m).astype(BF16))
                    ol = jnp.dot(p, v1, preferred_element_type=F32)
                    o_h = ol[:, 0:LANES]
                    o_pair = o_h if hh == 0 else jnp.where(first_head, o_pair, o_h)
                    m_all = jnp.where(lane == 2 * j + hh, m, m_all)
                    l_all = jnp.where(lane == 2 * j + hh, ol[:, LANES:], l_all)
                o_ref[r, rows, sl] = o_pair.astype(o_ref.dtype)
            m_ref[r, rows, :] = m_all
            l_ref[r, rows, :] = l_all


def _dilated_attention(qkv_g, g):
    N, dil, L, _ = qkv_g.shape
    tq = min(ATT_TQ, L)
    n_sub = min(ATT_TILES_PER_STEP, L // tq)
    n_seq = min(ATT_TILES_PER_STEP // n_sub, dil)
    bq = n_sub * tq
    nb = L // bq
    hb = bq // ATT_RADIUS
    n_halo = L // ATT_RADIUS

    def cur(c):
        return pl.BlockSpec((None, n_seq, bq, ATT_WIDTH), lambda n, r, i: (n, r, i, c))

    def before(c):
        return pl.BlockSpec((None, n_seq, ATT_RADIUS, ATT_WIDTH),
                            lambda n, r, i: (n, r, jnp.maximum(i * hb - 1, 0), c))

    def after(c):
        return pl.BlockSpec((None, n_seq, ATT_RADIUS, ATT_WIDTH),
                            lambda n, r, i: (n, r, jnp.minimum((i + 1) * hb, n_halo - 1), c))

    return pl.pallas_call(
        functools.partial(_attn_kernel, tq=tq, n_sub=n_sub, n_seq=n_seq, n_blocks=nb),
        grid=(N, dil // n_seq, nb),
        in_specs=[cur(0), before(1), cur(1), after(1), before(2), cur(2), after(2)],
        out_specs=[
            pl.BlockSpec((None, n_seq, bq, ATT_WIDTH), lambda n, r, i: (n, r, i, 0)),
            pl.BlockSpec((None, n_seq, bq, LANES), lambda n, r, i: (n, r, i, 0)),
            pl.BlockSpec((None, n_seq, bq, LANES), lambda n, r, i: (n, r, i, 0)),
        ],
        out_shape=[
            jax.ShapeDtypeStruct((N, dil, L, ATT_WIDTH), BF16),
            jax.ShapeDtypeStruct((N, dil, L, LANES), F32),
            jax.ShapeDtypeStruct((N, dil, L, LANES), F32),
        ],
        compiler_params=_params("parallel", "parallel", "parallel"),
        name=f"dilated_attn_{g}",
    )(*([qkv_g] * 7))


def _squared_relu_mlp(h, w1_ref, w2_ref):
    acc = None
    for c in range(D_FF // MLP_FF_TILE):
        sl = slice(c * MLP_FF_TILE, (c + 1) * MLP_FF_TILE)
        u = jnp.maximum(jnp.dot(h, w1_ref[:, sl], preferred_element_type=F32), 0.0)
        part = jnp.dot((u * u).astype(BF16), w2_ref[sl, :], preferred_element_type=F32)
        acc = part if acc is None else acc + part
    return acc


def _ab_tail_kernel(ap_ref, ac_ref, an_ref, *rest, tm, n_tiles):
    att_refs = [rest[3 * g:3 * g + 3] for g in range(N_PATTERNS)]
    (x_ref, mod_ref, cw_ref, cb_ref, lg_ref, lb_ref, wo_ref, gain_ref, w1_ref, w2_ref, o_ref,
     sh_ref, conv_ref, on_ref, mn_ref, ln_ref, act_ref) = rest[3 * N_PATTERNS:]
    i = pl.program_id(1)
    H = CONV_HALO
    SUB = 8

    sh_ref[0, 0:H, :] = jnp.where(i > 0, ap_ref[...].astype(F32), 0.0)
    sh_ref[0, H:H + tm, :] = ac_ref[...].astype(F32)
    sh_ref[0, H + tm:H + tm + H, :] = jnp.where(i < n_tiles - 1, an_ref[...].astype(F32), 0.0)
    span = tm + 2 * H - SUB
    for s in range(1, SUB):
        sh_ref[s, 0:span, :] = sh_ref[0, s:s + span, :]

    off = H - CONV_WIDTH // 2
    for c in range(CONV_CH // LANES):
        lanes = slice(c * LANES, (c + 1) * LANES)
        taps = [jnp.broadcast_to(cw_ref[j:j + 1, lanes], (CONV_ROWS, LANES)) for j in range(CONV_WIDTH)]
        bias = jnp.broadcast_to(cb_ref[:, lanes], (CONV_ROWS, LANES))
        for k in range(tm // CONV_ROWS):
            r0 = k * CONV_ROWS
            accs = [bias] + [None] * (CONV_CHAINS - 1)
            for j in range(CONV_WIDTH):
                s, blk = (off + j) % SUB, (off + j) // SUB
                term = taps[j] * sh_ref[s, pl.ds(r0 + SUB * blk, CONV_ROWS), lanes]
                a = accs[j % CONV_CHAINS]
                accs[j % CONV_CHAINS] = term if a is None else a + term
            while len(accs) > 1:
                accs = [accs[u] + accs[u + 1] for u in range(0, len(accs), 2)]
            conv_ref[pl.ds(r0, CONV_ROWS), lanes] = accs[0]

    def norm_chunk(k, carry):
        r0 = pl.multiple_of(k * NORM_ROWS, NORM_ROWS)
        acc = conv_ref[pl.ds(r0, NORM_ROWS), :]
        mu = jnp.mean(acc, axis=-1, keepdims=True)
        xc = acc - mu
        y = xc * lax.rsqrt(jnp.mean(xc * xc, axis=-1, keepdims=True) + EPS)
        y = y * lg_ref[...] + lb_ref[...]
        act_ref[pl.ds(r0, NORM_ROWS), :] = (y * jax.nn.sigmoid(y)).astype(act_ref.dtype)
        return carry

    lax.fori_loop(0, tm // NORM_ROWS, norm_chunk, 0)

    for g, (o_g, m_g, l_g) in enumerate(att_refs):
        dil = DILATED_PATTERNS[g][1]
        lt = tm // dil
        for r in range(dil):
            rows = pl.ds(r, lt, stride=dil) if dil > 1 else pl.ds(0, tm)
            o_r = o_g[r].astype(F32)
            for c in range(ATT_WIDTH // LANES):
                on_ref[g, c, rows, :] = o_r[:, c * LANES:(c + 1) * LANES]
            mn_ref[g, rows, :] = m_g[r]
            ln_ref[g, rows, :] = l_g[r]
    ms = [mn_ref[g] for g in range(N_PATTERNS)]
    mx = functools.reduce(jnp.maximum, ms)
    es = [jnp.exp2(m - mx) for m in ms]
    tot = functools.reduce(lambda a, b: a + b, [es[g] * ln_ref[g] for g in range(N_PATTERNS)])
    erow = lax.broadcasted_iota(jnp.int32, (LANES, ATT_WIDTH), 0)
    ecol = lax.broadcasted_iota(jnp.int32, (LANES, ATT_WIDTH), 1)
    expand = (ecol // ATT_HEAD_DIM == erow).astype(BF16)
    att = None
    for g in range(N_PATTERNS):
        wgt = es[g] / tot
        hi = wgt.astype(BF16)
        lo = (wgt - hi.astype(F32)).astype(BF16)
        wide = jnp.dot(hi, expand, preferred_element_type=F32) + jnp.dot(lo, expand, preferred_element_type=F32)
        term = wide * jnp.concatenate([on_ref[g, c] for c in range(ATT_WIDTH // LANES)], axis=1)
        att = term if att is None else att + term

    mix = jnp.dot(act_ref[...], wo_ref[0:CONV_CH, :], preferred_element_type=F32)
    mix = mix + jnp.dot(att.astype(BF16), wo_ref[CONV_CH:, :], preferred_element_type=F32)
    x1 = x_ref[...] + mod_ref[2:3, :] * mix
    h2 = _norm_mod(x1, gain_ref[...], mod_ref[4:5, :], mod_ref[3:4, :]).astype(BF16)
    o_ref[...] = x1 + mod_ref[5:6, :] * _squared_relu_mlp(h2, w1_ref, w2_ref)


def _ab_tail(a, att_parts, x, mod, n_off, conv_w, conv_b, ln_g, ln_b, w_out, gain, w1, w2):
    N, S, D = x.shape
    tm = min(ROW_TILE, S)
    nt = S // tm
    hb = tm // CONV_HALO
    n_halo = S // CONV_HALO
    row = lambda n, i: (n, i, 0)
    const = lambda n, i: (0, 0)
    once = pl.Buffered(1)
    att_specs, att_args = [], []
    for (_, dil), (o_g, m_g, l_g) in zip(DILATED_PATTERNS, att_parts):
        att_specs.append(pl.BlockSpec((None, dil, tm // dil, ATT_WIDTH), lambda n, i: (n, 0, i, 0)))
        att_specs.append(pl.BlockSpec((None, dil, tm // dil, LANES), lambda n, i: (n, 0, i, 0)))
        att_specs.append(pl.BlockSpec((None, dil, tm // dil, LANES), lambda n, i: (n, 0, i, 0)))
        att_args += [o_g, m_g, l_g]
    return pl.pallas_call(
        functools.partial(_ab_tail_kernel, tm=tm, n_tiles=nt),
        grid=(N, nt),
        in_specs=[
            pl.BlockSpec((None, CONV_HALO, CONV_CH), lambda n, i: (n, jnp.maximum(i * hb - 1, 0), 0)),
            pl.BlockSpec((None, tm, CONV_CH), row),
            pl.BlockSpec((None, CONV_HALO, CONV_CH), lambda n, i: (n, jnp.minimum((i + 1) * hb, n_halo - 1), 0)),
        ] + att_specs + [
            pl.BlockSpec((None, tm, D), row),
            pl.BlockSpec((None, None, 6, D), lambda n, i: (0, n + n_off, 0, 0)),
            pl.BlockSpec((CONV_WIDTH, CONV_CH), const),
            pl.BlockSpec((1, CONV_CH), const),
            pl.BlockSpec((1, CONV_CH), const),
            pl.BlockSpec((1, CONV_CH), const),
            pl.BlockSpec((D, D), const, pipeline_mode=once),
            pl.BlockSpec((1, D), const),
            pl.BlockSpec((D, D_FF), const, pipeline_mode=once),
            pl.BlockSpec((D_FF, D), const, pipeline_mode=once),
        ],
        out_specs=pl.BlockSpec((None, tm, D), row),
        out_shape=jax.ShapeDtypeStruct((N, S, D), F32),
        scratch_shapes=[
            pltpu.VMEM((8, tm + 2 * CONV_HALO, CONV_CH), F32),
            pltpu.VMEM((tm, CONV_CH), F32),
            pltpu.VMEM((N_PATTERNS, ATT_WIDTH // LANES, tm, LANES), F32),
            pltpu.VMEM((N_PATTERNS, tm, LANES), F32),
            pltpu.VMEM((N_PATTERNS, tm, LANES), F32),
            pltpu.VMEM((tm, CONV_CH), BF16),
        ],
        compiler_params=_params("parallel", "parallel"),
        name="ab_tail",
    )(a, a, a, *att_args, x, mod, conv_w, conv_b, ln_g, ln_b, w_out, gain, w1, w2)


def _c_tail_kernel(hf_ref, hb_ref, og_ref, x_ref, mod_ref, hn_ref, wo_ref, gain_ref, w1_ref, w2_ref, fg_ref, o_ref):
    dh = MLSTM_HEAD_DIM
    parts = []
    for hd in range(MLSTM_HEADS):
        sl = slice(hd * dh, (hd + 1) * dh)
        ht = hf_ref[:, sl].astype(F32) + hb_ref[:, sl].astype(F32)
        ht = ht * lax.rsqrt(jnp.mean(ht * ht, axis=-1, keepdims=True) + EPS)
        parts.append((og_ref[:, sl].astype(F32) * (ht * hn_ref[:, sl])).astype(BF16))
    z = jnp.concatenate(parts, axis=1)
    mix = jnp.dot(z, wo_ref[...], preferred_element_type=F32)
    x1 = x_ref[...] + mod_ref[2:3, :] * mix
    h2 = _norm_mod(x1, gain_ref[...], mod_ref[4:5, :], mod_ref[3:4, :]).astype(BF16)
    y = x1 + mod_ref[5:6, :] * _squared_relu_mlp(h2, w1_ref, w2_ref)
    o_ref[...] = y * lax.rsqrt(jnp.mean(y * y, axis=-1, keepdims=True) + EPS) * fg_ref[...]


def _c_tail(hf, hb, og, x, mod, n_off, head_norm, w_out, gain, w1, w2, final_gain):
    N, S, D = x.shape
    W = MLSTM_WIDTH
    tm = min(ROW_TILE, S)
    row = lambda n, i: (n, i, 0)
    const = lambda n, i: (0, 0)
    once = pl.Buffered(1)
    return pl.pallas_call(
        _c_tail_kernel,
        grid=(N, S // tm),
        in_specs=[
            pl.BlockSpec((None, tm, W), row),
            pl.BlockSpec((None, tm, W), row),
            pl.BlockSpec((None, tm, W), row),
            pl.BlockSpec((None, tm, D), row),
            pl.BlockSpec((None, None, 6, D), lambda n, i: (1, n + n_off, 0, 0)),
            pl.BlockSpec((1, W), const),
            pl.BlockSpec((W, D), const, pipeline_mode=once),
            pl.BlockSpec((1, D), const),
            pl.BlockSpec((D, D_FF), const, pipeline_mode=once),
            pl.BlockSpec((D_FF, D), const, pipeline_mode=once),
            pl.BlockSpec((1, D), const),
        ],
        out_specs=pl.BlockSpec((None, tm, D), row),
        out_shape=jax.ShapeDtypeStruct((N, S, D), F32),
        compiler_params=_params("parallel", "parallel"),
        name="c_tail",
    )(hf, hb, og, x, mod, head_norm, w_out, gain, w1, w2, final_gain)


def _c_in_kernel(x_ref, mod_ref, gain_ref, w_ref, wkt_ref, gb_ref, q_ref, kt_ref, v_ref, og_ref, gate_ref):
    W = MLSTM_WIDTH
    x = x_ref[...]
    h = _norm_mod(x, gain_ref[...], mod_ref[1:2, :], mod_ref[0:1, :]).astype(BF16)

    def seg(lo, width):
        return jnp.dot(h, w_ref[:, lo:lo + width], preferred_element_type=F32)

    q_ref[...] = seg(0, W).astype(q_ref.dtype)
    kt = lax.dot_general(wkt_ref[...], h, (((1,), (1,)), ((), ())), preferred_element_type=F32)
    kt_ref[...] = (kt * (MLSTM_HEAD_DIM ** -0.5)).astype(kt_ref.dtype)
    v_ref[...] = seg(2 * W, W).astype(v_ref.dtype)
    og_ref[...] = jax.nn.sigmoid(seg(3 * W, W)).astype(og_ref.dtype)
    z = seg(4 * W, GATE_PAD) + gb_ref[...]
    lane = lax.broadcasted_iota(jnp.int32, (1, GATE_PAD), 1)
    is_forget = (lane % (2 * MLSTM_HEADS)) >= MLSTM_HEADS
    log_sig = jnp.minimum(z, 0.0) - jnp.log1p(jnp.exp(-jnp.abs(z)))
    gate_ref[...] = jnp.where(is_forget, log_sig, z)


def _c_in_proj(x, mod, n_off, gain, w_in, wk_t, gate_b):
    N, S, D = x.shape
    W = MLSTM_WIDTH
    tm = min(ROW_TILE, S)
    row = lambda n, i: (n, i, 0)
    const = lambda n, i: (0, 0)
    return pl.pallas_call(
        _c_in_kernel,
        grid=(N, S // tm),
        in_specs=[
            pl.BlockSpec((None, tm, D), row),
            pl.BlockSpec((None, None, 6, D), lambda n, i: (1, n + n_off, 0, 0)),
            pl.BlockSpec((1, D), const),
            pl.BlockSpec((D, 4 * W + GATE_PAD), const),
            pl.BlockSpec((W, D), const),
            pl.BlockSpec((1, GATE_PAD), const),
        ],
        out_specs=[
            pl.BlockSpec((None, tm, W), row),
            pl.BlockSpec((None, W, tm), lambda n, i: (n, 0, i)),
            pl.BlockSpec((None, tm, W), row),
            pl.BlockSpec((None, tm, W), row),
            pl.BlockSpec((None, tm, GATE_PAD), row),
        ],
        out_shape=[
            jax.ShapeDtypeStruct((N, S, W), BF16),
            jax.ShapeDtypeStruct((N, W, S), BF16),
            jax.ShapeDtypeStruct((N, S, W), BF16),
            jax.ShapeDtypeStruct((N, S, W), BF16),
            jax.ShapeDtypeStruct((N, S, GATE_PAD), F32),
        ],
        compiler_params=_params("parallel", "parallel"),
        name="c_in_proj",
    )(x, mod, gain, w_in, wk_t, gate_b)


def _split3(x):
    hi = x.astype(BF16)
    r1 = x - hi.astype(F32)
    mid = r1.astype(BF16)
    lo = (r1 - mid.astype(F32)).astype(BF16)
    return hi, mid, lo


def _mlstm_kernel(qf_ref, kf_ref, vf_ref, gf_ref, qb_ref, kb_ref, vb_ref, gb_ref,
                  hf_ref, hb_ref, st_ref, m_ref, ml_ref, *, n_seq):
    L = MLSTM_CHUNK
    dh = MLSTM_HEAD_DIM
    H = MLSTM_HEADS

    @pl.when(pl.program_id(1) == 0)
    def _():
        st_ref[...] = jnp.zeros_like(st_ref)
        m_ref[...] = jnp.zeros_like(m_ref)
        ml_ref[...] = jnp.zeros_like(ml_ref)

    row = lax.broadcasted_iota(jnp.int32, (L, L), 0)
    col = lax.broadcasted_iota(jnp.int32, (L, L), 1)
    ones_blk = jnp.ones((L, dh), BF16)

    dirs = (
        (qf_ref, kf_ref, vf_ref, gf_ref, hf_ref, row >= col, L - 1),
        (qb_ref, kb_ref, vb_ref, gb_ref, hb_ref, row <= col, 0),
    )
    streams = [(b, d) for b in range(n_seq) for d in range(2)]
    pre = []
    for s, (b, d) in enumerate(streams):
        q_ref, kt_ref, v_ref, g_ref, h_ref, allowed, last = dirs[d]
        gates = g_ref[b]
        tri = allowed.astype(BF16)
        csum = sum(jnp.dot(tri, piece, preferred_element_type=F32) for piece in _split3(gates))
        gates_t = gates.T
        csum_t = csum.T
        lo = 2 * H * d
        ig_t = gates_t[lo:lo + H, :]
        b_t = csum_t[lo + H:lo + 2 * H, :]
        crow = ig_t - b_t
        b_last = b_t[:, last:last + 1]
        m_old = m_ref[s]
        wlog = b_last + crow
        m_new = jnp.maximum(b_last + m_old, jnp.max(wlog, axis=-1, keepdims=True))
        decay = jnp.exp(b_last + m_old - m_new)
        wrow = jnp.exp(wlog - m_new)
        m_ref[s] = m_new

        c_col = pltpu.roll(gates, H, 1) - csum
        run = c_col
        k = 1
        while k < L:
            pad = jnp.full((k, GATE_PAD), NEG_INF, F32)
            moved = (jnp.concatenate([pad, run[:L - k]], axis=0) if d == 0
                     else jnp.concatenate([run[k:], pad], axis=0))
            run = jnp.maximum(run, moved)
            k *= 2
        m_lane = ml_ref[s]
        inter_c = csum + m_lane
        mt_c = jnp.maximum(inter_c, csum + run)
        b_last_c = csum[last:last + 1, :]
        ml_ref[s] = jnp.maximum(b_last_c + m_lane, jnp.max(b_last_c + c_col, axis=0, keepdims=True))
        pre.append((csum - mt_c, jnp.exp(inter_c - mt_c), jnp.exp(-mt_c), crow, decay, wrow))

    units = [(s, hd) for hd in range(H) for s in range(len(streams))]

    stage_a = []
    for s, hd in units:
        b, d = streams[s]
        q_ref, kt_ref, v_ref, g_ref, h_ref, allowed, last = dirs[d]
        a_c, ei_c, emt_c, crow, decay, wrow = pre[s]
        sl = slice(hd * dh, (hd + 1) * dh)
        cf = 2 * H * d + H + hd
        q = q_ref[b, :, sl]
        kt = kt_ref[b, sl, :]
        column = lambda arr: jnp.broadcast_to(arr[:, cf:cf + 1], (L, L))
        e_intra = jnp.exp(jnp.where(allowed, column(a_c) + crow[hd:hd + 1, :], NEG_INF))
        sc = jnp.dot(q, kt, preferred_element_type=F32) * e_intra
        lhs = jnp.concatenate([sc.astype(BF16), (column(ei_c) * q.astype(F32)).astype(BF16)], axis=1)
        stage_a.append((lhs, column(emt_c)))

    for (s, hd), (lhs, emt) in zip(units, stage_a):
        b, d = streams[s]
        q_ref, kt_ref, v_ref, g_ref, h_ref, allowed, last = dirs[d]
        sl = slice(hd * dh, (hd + 1) * dh)
        v1 = jnp.concatenate([v_ref[b, :, sl], ones_blk], axis=1)
        rhs = jnp.concatenate([v1, st_ref[s, hd].astype(BF16)], axis=0)
        nd = jnp.dot(lhs, rhs, preferred_element_type=F32)
        den = jnp.maximum(jnp.abs(nd[:, dh:]), emt)
        h_ref[b, :, sl] = (nd[:, 0:dh] / den).astype(h_ref.dtype)

    for s, hd in units:
        b, d = streams[s]
        q_ref, kt_ref, v_ref, g_ref, h_ref, allowed, last = dirs[d]
        a_c, ei_c, emt_c, crow, decay, wrow = pre[s]
        sl = slice(hd * dh, (hd + 1) * dh)
        v1 = jnp.concatenate([v_ref[b, :, sl], ones_blk], axis=1)
        kw = (kt_ref[b, sl, :].astype(F32) * wrow[hd:hd + 1, :]).astype(BF16)
        dec = decay[hd:hd + 1, :]
        upd = jnp.dot(kw, v1, preferred_element_type=F32)
        st_ref[s, hd] = jnp.concatenate([dec, dec], axis=1) * st_ref[s, hd] + upd


def _mlstm_scan(q, kt, v, gates):
    N, S, W = q.shape
    L = MLSTM_CHUNK
    nc = S // L
    nb = min(MLSTM_SEQS_PER_STEP, N)
    rows_f = lambda n, i: (n, i, 0)
    rows_b = lambda n, i: (n, nc - 1 - i, 0)
    cols_f = lambda n, i: (n, 0, i)
    cols_b = lambda n, i: (n, 0, nc - 1 - i)
    blk = lambda im: pl.BlockSpec((nb, L, W), im)
    tblk = lambda im: pl.BlockSpec((nb, W, L), im)
    gblk = lambda im: pl.BlockSpec((nb, L, GATE_PAD), im)
    return pl.pallas_call(
        functools.partial(_mlstm_kernel, n_seq=nb),
        grid=(N // nb, nc),
        in_specs=[blk(rows_f), tblk(cols_f), blk(rows_f), gblk(rows_f),
                  blk(rows_b), tblk(cols_b), blk(rows_b), gblk(rows_b)],
        out_specs=[blk(rows_f), blk(rows_b)],
        out_shape=[jax.ShapeDtypeStruct((N, S, W), BF16)] * 2,
        scratch_shapes=[
            pltpu.VMEM((2 * nb, MLSTM_HEADS, MLSTM_HEAD_DIM, 2 * MLSTM_HEAD_DIM), F32),
            pltpu.VMEM((2 * nb, MLSTM_HEADS, LANES), F32),
            pltpu.VMEM((2 * nb, 1, GATE_PAD), F32),
        ],
        compiler_params=_params("parallel", "arbitrary"),
        name="mlstm_scan",
    )(q, kt, v, gates, q, kt, v, gates)


def _trunk(x, mod, n_off, p):
    N, S, D = x.shape
    tables = _rope_tables(S)
    a, *qkv = _ab_in_proj(x, mod, n_off, p["norm_mix0"], p["ab_w_in"], tables)
    att_parts = [_dilated_attention(qkv[g], g) for g in range(N_PATTERNS)]
    x = _ab_tail(a, att_parts, x, mod, n_off, p["conv_w"], p["conv_b"], p["conv_ln_g"], p["conv_ln_b"],
                 p["ab_w_out"], p["norm_mlp0"], p["mlp_w1_0"], p["mlp_w2_0"])
    q1, kt1, v1, og, gates = _c_in_proj(x, mod, n_off, p["norm_mix1"], p["c_w_in"], p["c_wk_t"], p["c_gate_b"])
    hf, hb = _mlstm_scan(q1, kt1, v1, gates)
    return _c_tail(hf, hb, og, x, mod, n_off, p["c_head_norm"], p["c_w_out"], p["norm_mlp1"],
                   p["mlp_w1_1"], p["mlp_w2_1"], p["norm_final"])


def kernel(x_prompt, x_sample, c_prompt, c_sample, ada_w, ada_b, norm_mix, norm_mlp, ab_w_in, conv_w, conv_b, conv_ln_g, conv_ln_b, ab_w_out, c_w_in, c_gate_b, c_head_norm, c_w_out, mlp_w1, mlp_w2, norm_final):
    D = D_MODEL
    n_p, n_s = c_prompt.shape[0], c_sample.shape[0]
    rows = n_p + n_s
    rows_pad = -(-rows // 8) * 8
    c_all = jnp.concatenate([c_prompt, c_sample, jnp.zeros((rows_pad - rows, D), F32)], axis=0)
    mod = _modulation(c_all, ada_w, ada_b)

    W = MLSTM_WIDTH
    pad = GATE_PAD - N_GATES
    p = {
        "norm_mix0": norm_mix[0].reshape(1, D),
        "norm_mix1": norm_mix[1].reshape(1, D),
        "norm_mlp0": norm_mlp[0].reshape(1, D),
        "norm_mlp1": norm_mlp[1].reshape(1, D),
        "ab_w_in": ab_w_in[0].astype(BF16),
        "conv_w": conv_w[0],
        "conv_b": conv_b[0].reshape(1, CONV_CH),
        "conv_ln_g": conv_ln_g[0].reshape(1, CONV_CH),
        "conv_ln_b": conv_ln_b[0].reshape(1, CONV_CH),
        "ab_w_out": ab_w_out[0].astype(BF16),
        "c_w_in": jnp.pad(c_w_in[0], ((0, 0), (0, pad))).astype(BF16),
        "c_wk_t": c_w_in[0][:, W:2 * W].T.astype(BF16),
        "c_gate_b": jnp.pad(c_gate_b[0], (0, pad)).reshape(1, GATE_PAD),
        "c_head_norm": c_head_norm[0].reshape(1, W),
        "c_w_out": c_w_out[0].astype(BF16),
        "mlp_w1_0": mlp_w1[0].astype(BF16),
        "mlp_w2_0": mlp_w2[0].astype(BF16),
        "mlp_w1_1": mlp_w1[1].astype(BF16),
        "mlp_w2_1": mlp_w2[1].astype(BF16),
        "norm_final": norm_final.reshape(1, D),
    }
    y_prompt = _trunk(x_prompt, mod, 0, p)
    y_sample = _trunk(x_sample, mod, n_p, p)
    return (y_prompt, y_sample)
```

```python
import functools

import jax
import jax.numpy as jnp
from jax import lax
from jax.experimental import pallas as pl
from jax.experimental.pallas import tpu as pltpu

D_MODEL = 1024
DEPTH = 2
CONV_CH = 512
CONV_WIDTH = 31
CONV_HALO = 16
ATT_HEADS = 8
ATT_HEAD_DIM = 64
ATT_WIDTH = ATT_HEADS * ATT_HEAD_DIM
DILATED_PATTERNS = ((128, 1), (512, 4), (2048, 16))
N_PATTERNS = len(DILATED_PATTERNS)
ATT_RADIUS = 64
ROPE_THETA = 500000.0
ROPE_DIM = ATT_HEAD_DIM // 4
ROPE_HALF = ROPE_DIM // 2
AB_IN = 2 * CONV_CH + N_PATTERNS * 3 * ATT_WIDTH
MLSTM_HEADS = 8
MLSTM_WIDTH = D_MODEL
MLSTM_HEAD_DIM = MLSTM_WIDTH // MLSTM_HEADS
MLSTM_CHUNK = 128
N_GATES = 4 * MLSTM_HEADS
GATE_PAD = 128
D_FF = 4 * D_MODEL
EPS = 1e-6
NEG_INF = -1e30
LOG2_E = 1.4426950408889634

LANES = 128
VMEM_LIMIT = 56 * 1024 * 1024
ROW_TILE = 512
PROJ_ROW_TILE = 1024
ATT_TQ = 128
ATT_TILES_PER_STEP = 8
CONV_ROWS = 32
CONV_CHAINS = 4
NORM_ROWS = 128
MLP_FF_TILE = 1024
MLSTM_SEQS_PER_STEP = 4

F32 = jnp.float32
BF16 = jnp.bfloat16


def _params(*sem):
    return pltpu.CompilerParams(dimension_semantics=sem, vmem_limit_bytes=VMEM_LIMIT)


def _norm_mod(x, gain, scale, shift):
    y = x * lax.rsqrt(jnp.mean(x * x, axis=-1, keepdims=True) + EPS)
    return (y * gain) * (1.0 + scale) + shift


def _mod_kernel(c_ref, w_ref, b_ref, o_ref):
    c = c_ref[...]
    s = c * jax.nn.sigmoid(c)
    o_ref[...] = jnp.dot(s, w_ref[...], preferred_element_type=F32,
                         precision=lax.Precision.HIGHEST) + b_ref[...]


def _modulation(c_all, ada_w, ada_b):
    R = c_all.shape[0]
    D = D_MODEL
    out = pl.pallas_call(
        _mod_kernel,
        grid=(DEPTH, 6),
        in_specs=[
            pl.BlockSpec((R, D), lambda l, j: (0, 0)),
            pl.BlockSpec((None, D, D), lambda l, j: (l, 0, j)),
            pl.BlockSpec((None, 1, D), lambda l, j: (l, 0, j)),
        ],
        out_specs=pl.BlockSpec((None, R, D), lambda l, j: (l, 0, j)),
        out_shape=jax.ShapeDtypeStruct((DEPTH, R, 6 * D), F32),
        compiler_params=_params("arbitrary", "arbitrary"),
        name="adaln_mod",
    )(c_all, ada_w, ada_b.reshape(DEPTH, 1, 6 * D))
    return out.reshape(DEPTH, R, 6, D)


def _rope_tables(S):
    inv_freq = jnp.power(ROPE_THETA, -jnp.arange(ROPE_HALF, dtype=F32) / ROPE_HALF)
    ang = jnp.arange(S).astype(F32)[:, None] * inv_freq[None, :]
    cos, sin = jnp.cos(ang), jnp.sin(ang)
    ones = jnp.ones((S, ATT_HEAD_DIM - ROPE_DIM), F32)
    zeros = jnp.zeros((S, ATT_HEAD_DIM - ROPE_DIM), F32)
    zh = jnp.zeros((S, ROPE_HALF), F32)
    cos_t = jnp.concatenate([cos, cos, ones], axis=1)
    sin_up = jnp.concatenate([-sin, zh, zeros], axis=1)
    sin_dn = jnp.concatenate([zh, sin, zeros], axis=1)
    rep = LANES // ATT_HEAD_DIM
    return tuple(jnp.tile(t, (1, rep)) for t in (cos_t, sin_up, sin_dn))


def _ab_in_kernel(x_ref, mod_ref, gain_ref, w_ref, *rest, tm):
    tab_refs = rest[:3]
    a_ref = rest[3]
    out_refs = rest[4:4 + N_PATTERNS]
    h_ref = rest[-1]
    hf = _norm_mod(x_ref[...], gain_ref[...], mod_ref[1:2, :], mod_ref[0:1, :])
    n_slab = hf.shape[1] // LANES
    for c in range(n_slab):
        h_ref[c] = hf[:, c * LANES:(c + 1) * LANES]
    h = hf.astype(BF16)

    def seg(lhs, j):
        return jnp.dot(lhs, w_ref[:, j * ATT_WIDTH:(j + 1) * ATT_WIDTH], preferred_element_type=F32)

    a_ref[...] = (seg(h, 0) * jax.nn.sigmoid(seg(h, 1))).astype(a_ref.dtype)

    def rope(p, tabs, scale):
        cos_t, sin_up, sin_dn = tabs
        parts = []
        for j in range(ATT_WIDTH // LANES):
            t = p[:, j * LANES:(j + 1) * LANES]
            up = pltpu.roll(t, LANES - ROPE_HALF, 1)
            dn = pltpu.roll(t, ROPE_HALF, 1)
            r = t * cos_t + up * sin_up + dn * sin_dn
            if scale != 1.0:
                r = r * scale
            parts.append(r.astype(BF16))
        return jnp.concatenate(parts, axis=1)

    for g, (_, dil) in enumerate(DILATED_PATTERNS):
        lt = tm // dil
        if dil == 1:
            hg = h
            tabs = tuple(t[...] for t in tab_refs)
        else:
            hg = jnp.concatenate(
                [jnp.concatenate([h_ref[c, pl.ds(r, lt, stride=dil), :] for c in range(n_slab)], axis=1)
                 for r in range(dil)], axis=0).astype(BF16)
            tabs = tuple(jnp.concatenate([t[pl.ds(r, lt, stride=dil), :] for r in range(dil)], axis=0)
                         for t in tab_refs)
        q = rope(seg(hg, 2 + 3 * g), tabs, ATT_HEAD_DIM ** -0.5 * LOG2_E)
        k = rope(seg(hg, 3 + 3 * g), tabs, 1.0)
        v = seg(hg, 4 + 3 * g).astype(BF16)
        for c, val in enumerate((q, k, v)):
            for r in range(dil):
                out_refs[g][r, :, c * ATT_WIDTH:(c + 1) * ATT_WIDTH] = val[r * lt:(r + 1) * lt, :]


def _ab_in_proj(x, mod, n_off, gain, w_in, tables):
    N, S, D = x.shape
    tm = min(PROJ_ROW_TILE, S)
    tab_spec = pl.BlockSpec((tm, LANES), lambda n, i: (i, 0))
    out_specs = [pl.BlockSpec((None, tm, CONV_CH), lambda n, i: (n, i, 0))]
    out_shape = [jax.ShapeDtypeStruct((N, S, CONV_CH), BF16)]
    for _, dil in DILATED_PATTERNS:
        out_specs.append(pl.BlockSpec((None, dil, tm // dil, 3 * ATT_WIDTH), lambda n, i: (n, 0, i, 0)))
        out_shape.append(jax.ShapeDtypeStruct((N, dil, S // dil, 3 * ATT_WIDTH), BF16))
    return pl.pallas_call(
        functools.partial(_ab_in_kernel, tm=tm),
        grid=(N, S // tm),
        in_specs=[
            pl.BlockSpec((None, tm, D), lambda n, i: (n, i, 0)),
            pl.BlockSpec((None, None, 6, D), lambda n, i: (0, n + n_off, 0, 0)),
            pl.BlockSpec((1, D), lambda n, i: (0, 0)),
            pl.BlockSpec((D, AB_IN), lambda n, i: (0, 0), pipeline_mode=pl.Buffered(1)),
        ] + [tab_spec] * 3,
        out_specs=out_specs,
        out_shape=out_shape,
        scratch_shapes=[pltpu.VMEM((D // LANES, tm, LANES), F32)],
        compiler_params=_params("parallel", "parallel"),
        name="ab_in_proj",
    )(x, mod, gain, w_in, *tables)


def _attn_kernel(q_ref, kp_ref, kc_ref, kn_ref, vp_ref, vc_ref, vn_ref, o_ref, m_ref, l_ref,
                 *, tq, n_sub, n_seq, n_blocks):
    i = pl.program_id(2)
    R = ATT_RADIUS
    bq = n_sub * tq
    tk = tq + 2 * R
    row = lax.broadcasted_iota(jnp.int32, (tq, tk), 0)
    col = lax.broadcasted_iota(jnp.int32, (tq, tk), 1)
    rel = col - row
    band = (rel >= 0) & (rel <= 2 * R)
    masks = []
    for t in range(n_sub):
        mask = band
        if t == 0:
            mask = mask & (col >= jnp.where(i == 0, R, 0))
        if t == n_sub - 1:
            mask = mask & (col < jnp.where(i == n_blocks - 1, tq + R, tk))
        masks.append(mask)
    lane = lax.broadcasted_iota(jnp.int32, (1, LANES), 1)
    first_head = lane < ATT_HEAD_DIM

    def window(p_ref, c_ref, n_ref, r, t, sl):
        lo, hi = t * tq - R, (t + 1) * tq + R
        parts = []
        if lo < 0:
            parts.append(p_ref[r, :, sl])
        parts.append(c_ref[r, max(lo, 0):min(hi, bq), sl])
        if hi > bq:
            parts.append(n_ref[r, :, sl])
        return jnp.concatenate(parts, axis=0)

    for r in range(n_seq):
        for t in range(n_sub):
            rows = slice(t * tq, (t + 1) * tq)
            m_all = jnp.zeros((tq, LANES), F32)
            l_all = jnp.ones((tq, LANES), F32)
            for j in range(ATT_WIDTH // LANES):
                sl = slice(j * LANES, (j + 1) * LANES)
                q = q_ref[r, rows, sl]
                kw = window(kp_ref, kc_ref, kn_ref, r, t, sl)
                vw = window(vp_ref, vc_ref, vn_ref, r, t, sl)
                v1 = jnp.concatenate([vw, jnp.ones_like(vw)], axis=1)
                o_pair = None
                for hh in range(2):
                    sel = first_head if hh == 0 else jnp.logical_not(first_head)
                    qm = jnp.where(sel, q, jnp.zeros_like(q))
                    s = lax.dot_general(qm, kw, (((1,), (1,)), ((), ())), preferred_element_type=F32)
                    s = jnp.where(masks[t], s, NEG_INF)
                    m = jnp.max(s, axis=-1, keepdims=True)
                    p = jnp.exp2((s - m).astype(BF16))
                    ol = jnp.dot(p, v1, preferred_element_type=F32)
                    o_h = ol[:, 0:LANES]
                    o_pair = o_h if hh == 0 else jnp.where(first_head, o_pair, o_h)
                    m_all = jnp.where(lane == 2 * j + hh, m, m_all)
                    l_all = jnp.where(lane == 2 * j + hh, ol[:, LANES:], l_all)
                o_ref[r, rows, sl] = o_pair.astype(o_ref.dtype)
            m_ref[r, rows, :] = m_all
            l_ref[r, rows, :] = l_all


def _dilated_attention(qkv_g, g):
    N, dil, L, _ = qkv_g.shape
    tq = min(ATT_TQ, L)
    n_sub = min(ATT_TILES_PER_STEP, L // tq)
    n_seq = min(ATT_TILES_PER_STEP // n_sub, dil)
    bq = n_sub * tq
    nb = L // bq
    hb = bq // ATT_RADIUS
    n_halo = L // ATT_RADIUS

    def cur(c):
        return pl.BlockSpec((None, n_seq, bq, ATT_WIDTH), lambda n, r, i: (n, r, i, c))

    def before(c):
        return pl.BlockSpec((None, n_seq, ATT_RADIUS, ATT_WIDTH),
                            lambda n, r, i: (n, r, jnp.maximum(i * hb - 1, 0), c))

    def after(c):
        return pl.BlockSpec((None, n_seq, ATT_RADIUS, ATT_WIDTH),
                            lambda n, r, i: (n, r, jnp.minimum((i + 1) * hb, n_halo - 1), c))

    return pl.pallas_call(
        functools.partial(_attn_kernel, tq=tq, n_sub=n_sub, n_seq=n_seq, n_blocks=nb),
        grid=(N, dil // n_seq, nb),
        in_specs=[cur(0), before(1), cur(1), after(1), before(2), cur(2), after(2)],
        out_specs=[
            pl.BlockSpec((None, n_seq, bq, ATT_WIDTH), lambda n, r, i: (n, r, i, 0)),
            pl.BlockSpec((None, n_seq, bq, LANES), lambda n, r, i: (n, r, i, 0)),
            pl.BlockSpec((None, n_seq, bq, LANES), lambda n, r, i: (n, r, i, 0)),
        ],
        out_shape=[
            jax.ShapeDtypeStruct((N, dil, L, ATT_WIDTH), BF16),
            jax.ShapeDtypeStruct((N, dil, L, LANES), F32),
            jax.ShapeDtypeStruct((N, dil, L, LANES), F32),
        ],
        compiler_params=_params("parallel", "parallel", "parallel"),
        name=f"dilated_attn_{g}",
    )(*([qkv_g] * 7))


def _squared_relu_mlp(h, w1_ref, w2_ref):
    acc = None
    for c in range(D_FF // MLP_FF_TILE):
        sl = slice(c * MLP_FF_TILE, (c + 1) * MLP_FF_TILE)
        u = jnp.maximum(jnp.dot(h, w1_ref[:, sl], preferred_element_type=F32), 0.0)
        part = jnp.dot((u * u).astype(BF16), w2_ref[sl, :], preferred_element_type=F32)
        acc = part if acc is None else acc + part
    return acc


def _ab_tail_kernel(ap_ref, ac_ref, an_ref, *rest, tm, n_tiles):
    att_refs = [rest[3 * g:3 * g + 3] for g in range(N_PATTERNS)]
    (x_ref, mod_ref, cw_ref, cb_ref, lg_ref, lb_ref, wo_ref, gain_ref, w1_ref, w2_ref, o_ref,
     sh_ref, conv_ref, on_ref, mn_ref, ln_ref, act_ref) = rest[3 * N_PATTERNS:]
    i = pl.program_id(1)
    H = CONV_HALO
    SUB = 8

    sh_ref[0, 0:H, :] = jnp.where(i > 0, ap_ref[...].astype(F32), 0.0)
    sh_ref[0, H:H + tm, :] = ac_ref[...].astype(F32)
    sh_ref[0, H + tm:H + tm + H, :] = jnp.where(i < n_tiles - 1, an_ref[...].astype(F32), 0.0)
    span = tm + 2 * H - SUB
    for s in range(1, SUB):
        sh_ref[s, 0:span, :] = sh_ref[0, s:s + span, :]

    off = H - CONV_WIDTH // 2
    for c in range(CONV_CH // LANES):
        lanes = slice(c * LANES, (c + 1) * LANES)
        taps = [jnp.broadcast_to(cw_ref[j:j + 1, lanes], (CONV_ROWS, LANES)) for j in range(CONV_WIDTH)]
        bias = jnp.broadcast_to(cb_ref[:, lanes], (CONV_ROWS, LANES))
        for k in range(tm // CONV_ROWS):
            r0 = k * CONV_ROWS
            accs = [bias] + [None] * (CONV_CHAINS - 1)
            for j in range(CONV_WIDTH):
                s, blk = (off + j) % SUB, (off + j) // SUB
                term = taps[j] * sh_ref[s, pl.ds(r0 + SUB * blk, CONV_ROWS), lanes]
                a = accs[j % CONV_CHAINS]
                accs[j % CONV_CHAINS] = term if a is None else a + term
            while len(accs) > 1:
                accs = [accs[u] + accs[u + 1] for u in range(0, len(accs), 2)]
            conv_ref[pl.ds(r0, CONV_ROWS), lanes] = accs[0]

    def norm_chunk(k, carry):
        r0 = pl.multiple_of(k * NORM_ROWS, NORM_ROWS)
        acc = conv_ref[pl.ds(r0, NORM_ROWS), :]
        mu = jnp.mean(acc, axis=-1, keepdims=True)
        xc = acc - mu
        y = xc * lax.rsqrt(jnp.mean(xc * xc, axis=-1, keepdims=True) + EPS)
        y = y * lg_ref[...] + lb_ref[...]
        act_ref[pl.ds(r0, NORM_ROWS), :] = (y * jax.nn.sigmoid(y)).astype(act_ref.dtype)
        return carry

    lax.fori_loop(0, tm // NORM_ROWS, norm_chunk, 0)

    for g, (o_g, m_g, l_g) in enumerate(att_refs):
        dil = DILATED_PATTERNS[g][1]
        lt = tm // dil
        for r in range(dil):
            rows = pl.ds(r, lt, stride=dil) if dil > 1 else pl.ds(0, tm)
            o_r = o_g[r].astype(F32)
            for c in range(ATT_WIDTH // LANES):
                on_ref[g, c, rows, :] = o_r[:, c * LANES:(c + 1) * LANES]
            mn_ref[g, rows, :] = m_g[r]
            ln_ref[g, rows, :] = l_g[r]
    ms = [mn_ref[g] for g in range(N_PATTERNS)]
    mx = functools.reduce(jnp.maximum, ms)
    es = [jnp.exp2(m - mx) for m in ms]
    tot = functools.reduce(lambda a, b: a + b, [es[g] * ln_ref[g] for g in range(N_PATTERNS)])
    erow = lax.broadcasted_iota(jnp.int32, (LANES, ATT_WIDTH), 0)
    ecol = lax.broadcasted_iota(jnp.int32, (LANES, ATT_WIDTH), 1)
    expand = (ecol // ATT_HEAD_DIM == erow).astype(BF16)
    att = None
    for g in range(N_PATTERNS):
        wgt = es[g] / tot
        hi = wgt.astype(BF16)
        lo = (wgt - hi.astype(F32)).astype(BF16)
        wide = jnp.dot(hi, expand, preferred_element_type=F32) + jnp.dot(lo, expand, preferred_element_type=F32)
        term = wide * jnp.concatenate([on_ref[g, c] for c in range(ATT_WIDTH // LANES)], axis=1)
        att = term if att is None else att + term

    mix = jnp.dot(act_ref[...], wo_ref[0:CONV_CH, :], preferred_element_type=F32)
    mix = mix + jnp.dot(att.astype(BF16), wo_ref[CONV_CH:, :], preferred_element_type=F32)
    x1 = x_ref[...] + mod_ref[2:3, :] * mix
    h2 = _norm_mod(x1, gain_ref[...], mod_ref[4:5, :], mod_ref[3:4, :]).astype(BF16)
    o_ref[...] = x1 + mod_ref[5:6, :] * _squared_relu_mlp(h2, w1_ref, w2_ref)


def _ab_tail(a, att_parts, x, mod, n_off, conv_w, conv_b, ln_g, ln_b, w_out, gain, w1, w2):
    N, S, D = x.shape
    tm = min(ROW_TILE, S)
    nt = S // tm
    hb = tm // CONV_HALO
    n_halo = S // CONV_HALO
    row = lambda n, i: (n, i, 0)
    const = lambda n, i: (0, 0)
    once = pl.Buffered(1)
    att_specs, att_args = [], []
    for (_, dil), (o_g, m_g, l_g) in zip(DILATED_PATTERNS, att_parts):
        att_specs.append(pl.BlockSpec((None, dil, tm // dil, ATT_WIDTH), lambda n, i: (n, 0, i, 0)))
        att_specs.append(pl.BlockSpec((None, dil, tm // dil, LANES), lambda n, i: (n, 0, i, 0)))
        att_specs.append(pl.BlockSpec((None, dil, tm // dil, LANES), lambda n, i: (n, 0, i, 0)))
        att_args += [o_g, m_g, l_g]
    return pl.pallas_call(
        functools.partial(_ab_tail_kernel, tm=tm, n_tiles=nt),
        grid=(N, nt),
        in_specs=[
            pl.BlockSpec((None, CONV_HALO, CONV_CH), lambda n, i: (n, jnp.maximum(i * hb - 1, 0), 0)),
            pl.BlockSpec((None, tm, CONV_CH), row),
            pl.BlockSpec((None, CONV_HALO, CONV_CH), lambda n, i: (n, jnp.minimum((i + 1) * hb, n_halo - 1), 0)),
        ] + att_specs + [
            pl.BlockSpec((None, tm, D), row),
            pl.BlockSpec((None, None, 6, D), lambda n, i: (0, n + n_off, 0, 0)),
            pl.BlockSpec((CONV_WIDTH, CONV_CH), const),
            pl.BlockSpec((1, CONV_CH), const),
            pl.BlockSpec((1, CONV_CH), const),
            pl.BlockSpec((1, CONV_CH), const),
            pl.BlockSpec((D, D), const, pipeline_mode=once),
            pl.BlockSpec((1, D), const),
            pl.BlockSpec((D, D_FF), const, pipeline_mode=once),
            pl.BlockSpec((D_FF, D), const, pipeline_mode=once),
        ],
        out_specs=pl.BlockSpec((None, tm, D), row),
        out_shape=jax.ShapeDtypeStruct((N, S, D), F32),
        scratch_shapes=[
            pltpu.VMEM((8, tm + 2 * CONV_HALO, CONV_CH), F32),
            pltpu.VMEM((tm, CONV_CH), F32),
            pltpu.VMEM((N_PATTERNS, ATT_WIDTH // LANES, tm, LANES), F32),
            pltpu.VMEM((N_PATTERNS, tm, LANES), F32),
            pltpu.VMEM((N_PATTERNS, tm, LANES), F32),
            pltpu.VMEM((tm, CONV_CH), BF16),
        ],
        compiler_params=_params("parallel", "parallel"),
        name="ab_tail",
    )(a, a, a, *att_args, x, mod, conv_w, conv_b, ln_g, ln_b, w_out, gain, w1, w2)


def _c_tail_kernel(hf_ref, hb_ref, og_ref, x_ref, mod_ref, hn_ref, wo_ref, gain_ref, w1_ref, w2_ref, fg_ref, o_ref):
    dh = MLSTM_HEAD_DIM
    parts = []
    for hd in range(MLSTM_HEADS):
        sl = slice(hd * dh, (hd + 1) * dh)
        ht = hf_ref[:, sl].astype(F32) + hb_ref[:, sl].astype(F32)
        ht = ht * lax.rsqrt(jnp.mean(ht * ht, axis=-1, keepdims=True) + EPS)
        parts.append((og_ref[:, sl].astype(F32) * (ht * hn_ref[:, sl])).astype(BF16))
    z = jnp.concatenate(parts, axis=1)
    mix = jnp.dot(z, wo_ref[...], preferred_element_type=F32)
    x1 = x_ref[...] + mod_ref[2:3, :] * mix
    h2 = _norm_mod(x1, gain_ref[...], mod_ref[4:5, :], mod_ref[3:4, :]).astype(BF16)
    y = x1 + mod_ref[5:6, :] * _squared_relu_mlp(h2, w1_ref, w2_ref)
    o_ref[...] = y * lax.rsqrt(jnp.mean(y * y, axis=-1, keepdims=True) + EPS) * fg_ref[...]


def _c_tail(hf, hb, og, x, mod, n_off, head_norm, w_out, gain, w1, w2, final_gain):
    N, S, D = x.shape
    W = MLSTM_WIDTH
    tm = min(ROW_TILE, S)
    row = lambda n, i: (n, i, 0)
    const = lambda n, i: (0, 0)
    once = pl.Buffered(1)
    return pl.pallas_call(
        _c_tail_kernel,
        grid=(N, S // tm),
        in_specs=[
            pl.BlockSpec((None, tm, W), row),
            pl.BlockSpec((None, tm, W), row),
            pl.BlockSpec((None, tm, W), row),
            pl.BlockSpec((None, tm, D), row),
            pl.BlockSpec((None, None, 6, D), lambda n, i: (1, n + n_off, 0, 0)),
            pl.BlockSpec((1, W), const),
            pl.BlockSpec((W, D), const, pipeline_mode=once),
            pl.BlockSpec((1, D), const),
            pl.BlockSpec((D, D_FF), const, pipeline_mode=once),
            pl.BlockSpec((D_FF, D), const, pipeline_mode=once),
            pl.BlockSpec((1, D), const),
        ],
        out_specs=pl.BlockSpec((None, tm, D), row),
        out_shape=jax.ShapeDtypeStruct((N, S, D), F32),
        compiler_params=_params("parallel", "parallel"),
        name="c_tail",
    )(hf, hb, og, x, mod, head_norm, w_out, gain, w1, w2, final_gain)


def _c_in_kernel(x_ref, mod_ref, gain_ref, w_ref, wkt_ref, gb_ref, q_ref, kt_ref, v_ref, og_ref, gate_ref):
    W = MLSTM_WIDTH
    x = x_ref[...]
    h = _norm_mod(x, gain_ref[...], mod_ref[1:2, :], mod_ref[0:1, :]).astype(BF16)

    def seg(lo, width):
        return jnp.dot(h, w_ref[:, lo:lo + width], preferred_element_type=F32)

    q_ref[...] = seg(0, W).astype(q_ref.dtype)
    kt = lax.dot_general(wkt_ref[...], h, (((1,), (1,)), ((), ())), preferred_element_type=F32)
    kt_ref[...] = (kt * (MLSTM_HEAD_DIM ** -0.5)).astype(kt_ref.dtype)
    v_ref[...] = seg(2 * W, W).astype(v_ref.dtype)
    og_ref[...] = jax.nn.sigmoid(seg(3 * W, W)).astype(og_ref.dtype)
    z = seg(4 * W, GATE_PAD) + gb_ref[...]
    lane = lax.broadcasted_iota(jnp.int32, (1, GATE_PAD), 1)
    is_forget = (lane % (2 * MLSTM_HEADS)) >= MLSTM_HEADS
    log_sig = jnp.minimum(z, 0.0) - jnp.log1p(jnp.exp(-jnp.abs(z)))
    gate_ref[...] = jnp.where(is_forget, log_sig, z)


def _c_in_proj(x, mod, n_off, gain, w_in, wk_t, gate_b):
    N, S, D = x.shape
    W = MLSTM_WIDTH
    tm = min(PROJ_ROW_TILE, S)
    row = lambda n, i: (n, i, 0)
    const = lambda n, i: (0, 0)
    return pl.pallas_call(
        _c_in_kernel,
        grid=(N, S // tm),
        in_specs=[
            pl.BlockSpec((None, tm, D), row),
            pl.BlockSpec((None, None, 6, D), lambda n, i: (1, n + n_off, 0, 0)),
            pl.BlockSpec((1, D), const),
            pl.BlockSpec((D, 4 * W + GATE_PAD), const, pipeline_mode=pl.Buffered(1)),
            pl.BlockSpec((W, D), const, pipeline_mode=pl.Buffered(1)),
            pl.BlockSpec((1, GATE_PAD), const),
        ],
        out_specs=[
            pl.BlockSpec((None, tm, W), row),
            pl.BlockSpec((None, W, tm), lambda n, i: (n, 0, i)),
            pl.BlockSpec((None, tm, W), row),
            pl.BlockSpec((None, tm, W), row),
            pl.BlockSpec((None, tm, GATE_PAD), row),
        ],
        out_shape=[
            jax.ShapeDtypeStruct((N, S, W), BF16),
            jax.ShapeDtypeStruct((N, W, S), BF16),
            jax.ShapeDtypeStruct((N, S, W), BF16),
            jax.ShapeDtypeStruct((N, S, W), BF16),
            jax.ShapeDtypeStruct((N, S, GATE_PAD), F32),
        ],
        compiler_params=_params("parallel", "parallel"),
        name="c_in_proj",
    )(x, mod, gain, w_in, wk_t, gate_b)


def _split3(x):
    hi = x.astype(BF16)
    r1 = x - hi.astype(F32)
    mid = r1.astype(BF16)
    lo = (r1 - mid.astype(F32)).astype(BF16)
    return hi, mid, lo


def _mlstm_kernel(qf_ref, kf_ref, vf_ref, gf_ref, qb_ref, kb_ref, vb_ref, gb_ref,
                  hf_ref, hb_ref, st_ref, m_ref, ml_ref, *, n_seq):
    L = MLSTM_CHUNK
    dh = MLSTM_HEAD_DIM
    H = MLSTM_HEADS

    @pl.when(pl.program_id(1) == 0)
    def _():
        st_ref[...] = jnp.zeros_like(st_ref)
        m_ref[...] = jnp.zeros_like(m_ref)
        ml_ref[...] = jnp.zeros_like(ml_ref)

    row = lax.broadcasted_iota(jnp.int32, (L, L), 0)
    col = lax.broadcasted_iota(jnp.int32, (L, L), 1)
    ones_blk = jnp.ones((L, dh), BF16)

    dirs = (
        (qf_ref, kf_ref, vf_ref, gf_ref, hf_ref, row >= col, L - 1),
        (qb_ref, kb_ref, vb_ref, gb_ref, hb_ref, row <= col, 0),
    )
    streams = [(b, d) for b in range(n_seq) for d in range(2)]
    pre = []
    for s, (b, d) in enumerate(streams):
        q_ref, kt_ref, v_ref, g_ref, h_ref, allowed, last = dirs[d]
        gates = g_ref[b]
        tri = allowed.astype(BF16)
        csum = sum(jnp.dot(tri, piece, preferred_element_type=F32) for piece in _split3(gates))
        gates_t = gates.T
        csum_t = csum.T
        lo = 2 * H * d
        ig_t = gates_t[lo:lo + H, :]
        b_t = csum_t[lo + H:lo + 2 * H, :]
        crow = ig_t - b_t
        b_last = b_t[:, last:last + 1]
        m_old = m_ref[s]
        wlog = b_last + crow
        m_new = jnp.maximum(b_last + m_old, jnp.max(wlog, axis=-1, keepdims=True))
        decay = jnp.exp(b_last + m_old - m_new)
        wrow = jnp.exp(wlog - m_new)
        m_ref[s] = m_new

        c_col = pltpu.roll(gates, H, 1) - csum
        run = c_col
        k = 1
        while k < L:
            pad = jnp.full((k, GATE_PAD), NEG_INF, F32)
            moved = (jnp.concatenate([pad, run[:L - k]], axis=0) if d == 0
                     else jnp.concatenate([run[k:], pad], axis=0))
            run = jnp.maximum(run, moved)
            k *= 2
        m_lane = ml_ref[s]
        inter_c = csum + m_lane
        mt_c = jnp.maximum(inter_c, csum + run)
        b_last_c = csum[last:last + 1, :]
        ml_ref[s] = jnp.maximum(b_last_c + m_lane, jnp.max(b_last_c + c_col, axis=0, keepdims=True))
        pre.append((csum - mt_c, jnp.exp(inter_c - mt_c), jnp.exp(-mt_c), crow, decay, wrow))

    units = [(s, hd) for hd in range(H) for s in range(len(streams))]

    stage_a = []
    for s, hd in units:
        b, d = streams[s]
        q_ref, kt_ref, v_ref, g_ref, h_ref, allowed, last = dirs[d]
        a_c, ei_c, emt_c, crow, decay, wrow = pre[s]
        sl = slice(hd * dh, (hd + 1) * dh)
        cf = 2 * H * d + H + hd
        q = q_ref[b, :, sl]
        kt = kt_ref[b, sl, :]
        column = lambda arr: jnp.broadcast_to(arr[:, cf:cf + 1], (L, L))
        e_intra = jnp.exp(jnp.where(allowed, column(a_c) + crow[hd:hd + 1, :], NEG_INF))
        sc = jnp.dot(q, kt, preferred_element_type=F32) * e_intra
        lhs = jnp.concatenate([sc.astype(BF16), (column(ei_c) * q.astype(F32)).astype(BF16)], axis=1)
        stage_a.append((lhs, column(emt_c)))

    for (s, hd), (lhs, emt) in zip(units, stage_a):
        b, d = streams[s]
        q_ref, kt_ref, v_ref, g_ref, h_ref, allowed, last = dirs[d]
        sl = slice(hd * dh, (hd + 1) * dh)
        v1 = jnp.concatenate([v_ref[b, :, sl], ones_blk], axis=1)
        rhs = jnp.concatenate([v1, st_ref[s, hd].astype(BF16)], axis=0)
        nd = jnp.dot(lhs, rhs, preferred_element_type=F32)
        den = jnp.maximum(jnp.abs(nd[:, dh:]), emt)
        h_ref[b, :, sl] = (nd[:, 0:dh] / den).astype(h_ref.dtype)

    for s, hd in units:
        b, d = streams[s]
        q_ref, kt_ref, v_ref, g_ref, h_ref, allowed, last = dirs[d]
        a_c, ei_c, emt_c, crow, decay, wrow = pre[s]
        sl = slice(hd * dh, (hd + 1) * dh)
        v1 = jnp.concatenate([v_ref[b, :, sl], ones_blk], axis=1)
        kw = (kt_ref[b, sl, :].astype(F32) * wrow[hd:hd + 1, :]).astype(BF16)
        dec = decay[hd:hd + 1, :]
        upd = jnp.dot(kw, v1, preferred_element_type=F32)
        st_ref[s, hd] = jnp.concatenate([dec, dec], axis=1) * st_ref[s, hd] + upd


def _mlstm_scan(q, kt, v, gates):
    N, S, W = q.shape
    L = MLSTM_CHUNK
    nc = S // L
    nb = min(MLSTM_SEQS_PER_STEP, N)
    rows_f = lambda n, i: (n, i, 0)
    rows_b = lambda n, i: (n, nc - 1 - i, 0)
    cols_f = lambda n, i: (n, 0, i)
    cols_b = lambda n, i: (n, 0, nc - 1 - i)
    blk = lambda im: pl.BlockSpec((nb, L, W), im)
    tblk = lambda im: pl.BlockSpec((nb, W, L), im)
    gblk = lambda im: pl.BlockSpec((nb, L, GATE_PAD), im)
    return pl.pallas_call(
        functools.partial(_mlstm_kernel, n_seq=nb),
        grid=(N // nb, nc),
        in_specs=[blk(rows_f), tblk(cols_f), blk(rows_f), gblk(rows_f),
                  blk(rows_b), tblk(cols_b), blk(rows_b), gblk(rows_b)],
        out_specs=[blk(rows_f), blk(rows_b)],
        out_shape=[jax.ShapeDtypeStruct((N, S, W), BF16)] * 2,
        scratch_shapes=[
            pltpu.VMEM((2 * nb, MLSTM_HEADS, MLSTM_HEAD_DIM, 2 * MLSTM_HEAD_DIM), F32),
            pltpu.VMEM((2 * nb, MLSTM_HEADS, LANES), F32),
            pltpu.VMEM((2 * nb, 1, GATE_PAD), F32),
        ],
        compiler_params=_params("parallel", "arbitrary"),
        name="mlstm_scan",
    )(q, kt, v, gates, q, kt, v, gates)


def _trunk(x, mod, n_off, p):
    N, S, D = x.shape
    tables = _rope_tables(S)
    a, *qkv = _ab_in_proj(x, mod, n_off, p["norm_mix0"], p["ab_w_in"], tables)
    att_parts = [_dilated_attention(qkv[g], g) for g in range(N_PATTERNS)]
    x = _ab_tail(a, att_parts, x, mod, n_off, p["conv_w"], p["conv_b"], p["conv_ln_g"], p["conv_ln_b"],
                 p["ab_w_out"], p["norm_mlp0"], p["mlp_w1_0"], p["mlp_w2_0"])
    q1, kt1, v1, og, gates = _c_in_proj(x, mod, n_off, p["norm_mix1"], p["c_w_in"], p["c_wk_t"], p["c_gate_b"])
    hf, hb = _mlstm_scan(q1, kt1, v1, gates)
    return _c_tail(hf, hb, og, x, mod, n_off, p["c_head_norm"], p["c_w_out"], p["norm_mlp1"],
                   p["mlp_w1_1"], p["mlp_w2_1"], p["norm_final"])


def kernel(x_prompt, x_sample, c_prompt, c_sample, ada_w, ada_b, norm_mix, norm_mlp, ab_w_in, conv_w, conv_b, conv_ln_g, conv_ln_b, ab_w_out, c_w_in, c_gate_b, c_head_norm, c_w_out, mlp_w1, mlp_w2, norm_final):
    D = D_MODEL
    n_p, n_s = c_prompt.shape[0], c_sample.shape[0]
    rows = n_p + n_s
    rows_pad = -(-rows // 8) * 8
    c_all = jnp.concatenate([c_prompt, c_sample, jnp.zeros((rows_pad - rows, D), F32)], axis=0)
    mod = _modulation(c_all, ada_w, ada_b)

    W = MLSTM_WIDTH
    pad = GATE_PAD - N_GATES
    p = {
        "norm_mix0": norm_mix[0].reshape(1, D),
        "norm_mix1": norm_mix[1].reshape(1, D),
        "norm_mlp0": norm_mlp[0].reshape(1, D),
        "norm_mlp1": norm_mlp[1].reshape(1, D),
        "ab_w_in": ab_w_in[0].astype(BF16),
        "conv_w": conv_w[0],
        "conv_b": conv_b[0].reshape(1, CONV_CH),
        "conv_ln_g": conv_ln_g[0].reshape(1, CONV_CH),
        "conv_ln_b": conv_ln_b[0].reshape(1, CONV_CH),
        "ab_w_out": ab_w_out[0].astype(BF16),
        "c_w_in": jnp.pad(c_w_in[0], ((0, 0), (0, pad))).astype(BF16),
        "c_wk_t": c_w_in[0][:, W:2 * W].T.astype(BF16),
        "c_gate_b": jnp.pad(c_gate_b[0], (0, pad)).reshape(1, GATE_PAD),
        "c_head_norm": c_head_norm[0].reshape(1, W),
        "c_w_out": c_w_out[0].astype(BF16),
        "mlp_w1_0": mlp_w1[0].astype(BF16),
        "mlp_w2_0": mlp_w2[0].astype(BF16),
        "mlp_w1_1": mlp_w1[1].astype(BF16),
        "mlp_w2_1": mlp_w2[1].astype(BF16),
        "norm_final": norm_final.reshape(1, D),
    }
    y_prompt = _trunk(x_prompt, mod, 0, p)
    y_sample = _trunk(x_sample, mod, n_p, p)
    return (y_prompt, y_sample)
```
